```python
import jax, jax.numpy as jnp
from jax import lax
import numpy as np

D_MODEL = 4096
BATCH = 16
SEQ = 256
DEPTH = 4
DEC_BATCH = 4
DEC_SEQ = 4096
PAST_LEN = 512

GRID_W = 64
N_HEADS = 16
N_KV_HEADS = 4
GROUP = N_HEADS // N_KV_HEADS
HEAD_DIM = 128
ATTN_WIDTH = N_HEADS * HEAD_DIM
KV_WIDTH = N_KV_HEADS * HEAD_DIM
CONV_WIDTH = D_MODEL - ATTN_WIDTH
IN_WIDTH = ATTN_WIDTH + 2 * KV_WIDTH + 3 * CONV_WIDTH
CONV_K = 3
WINDOW = 128
BLOCK = 128
D_FF = 11008
ROPE_THETA = 10000.0
EPS = 1e-6
NEG = -1e30

kernel_name = 'hybrid_conv_swa_diffusion_step'


def rmsnorm(x, g):
    x32 = x.astype(jnp.float32)
    y = x32 * lax.rsqrt(jnp.mean(x32 * x32, axis=-1, keepdims=True) + EPS)
    return (y * g.astype(jnp.float32)).astype(x.dtype)


def adaln_params(cvec, w_mod, b_mod):
    m = jax.nn.silu(cvec) @ w_mod + b_mod
    return jnp.split(m[:, None, :], 6, axis=-1)


def modulate(h, shift, scale):
    return h * (1 + scale) + shift


def dwconv3(x, w):
    xp = jnp.pad(x, ((0, 0), (1, 1), (0, 0)))
    return w[0] * xp[:, :-2] + w[1] * xp[:, 1:-1] + w[2] * xp[:, 2:]


def axial_rope_tables(t_len):
    rows = t_len // GRID_W
    row = jnp.repeat(jnp.arange(rows), GRID_W).astype(jnp.float32)
    col = jnp.tile(jnp.arange(GRID_W), rows).astype(jnp.float32)
    n_freq = HEAD_DIM // 4
    inv = ROPE_THETA ** (-jnp.arange(n_freq, dtype=jnp.float32) / n_freq)
    ar = row[:, None] * inv
    ac = col[:, None] * inv
    ang = jnp.concatenate([ar, ar, ac, ac], axis=-1)
    return jnp.cos(ang), jnp.sin(ang)


def apply_axial_rope(x, cos, sin):
    x32 = x.astype(jnp.float32)
    half, q4 = HEAD_DIM // 2, HEAD_DIM // 4
    xr, xc = x32[..., :half], x32[..., half:]
    rot = jnp.concatenate([-xr[..., q4:], xr[..., :q4], -xc[..., q4:], xc[..., :q4]], axis=-1)
    return (x32 * cos[:, None, :] + rot * sin[:, None, :]).astype(x.dtype)


def sink_softmax_pv(s, v, sink):
    sk = sink.astype(jnp.float32).reshape(1, N_KV_HEADS, GROUP, 1, 1)
    m = jnp.maximum(jnp.max(s, axis=-1, keepdims=True), sk)
    e = jnp.exp(s - m)
    p = e / (jnp.sum(e, axis=-1, keepdims=True) + jnp.exp(sk - m))
    return jnp.einsum('bkgqs,bskd->bqkgd', p.astype(v.dtype), v)


def context_attention(q, k, v, sink):
    b, s_len = q.shape[:2]
    nb = s_len // BLOCK
    scale = HEAD_DIM ** -0.5
    qb = q.reshape(b, nb, BLOCK, N_KV_HEADS, GROUP, HEAD_DIM).swapaxes(0, 1)

    def one(qblk):
        s = jnp.einsum('bqkgd,bskd->bkgqs', qblk, k).astype(jnp.float32) * scale
        return sink_softmax_pv(s, v, sink)

    out = lax.map(one, qb)
    return out.swapaxes(0, 1).reshape(b, s_len, ATTN_WIDTH)


def latent_attention(q, k, v, k_ctx, v_ctx, sink):
    b, t_len = q.shape[:2]
    nb = t_len // BLOCK
    scale = HEAD_DIM ** -0.5
    qb = q.reshape(b, nb, BLOCK, N_KV_HEADS, GROUP, HEAD_DIM).swapaxes(0, 1)
    pad = ((0, 0), (BLOCK, BLOCK), (0, 0), (0, 0))
    kp = jnp.pad(k, pad)
    vp = jnp.pad(v, pad)
    qi = jnp.arange(BLOCK)[:, None]
    kj = jnp.arange(3 * BLOCK)[None, :]
    rel = kj - BLOCK - qi

    def one(args):
        blk, qblk = args
        kb = lax.dynamic_slice_in_dim(kp, blk * BLOCK, 3 * BLOCK, axis=1)
        vb = lax.dynamic_slice_in_dim(vp, blk * BLOCK, 3 * BLOCK, axis=1)
        kpos = (blk - 1) * BLOCK + kj
        valid = (jnp.abs(rel) <= WINDOW) & (kpos >= 0) & (kpos < t_len)
        s_band = jnp.einsum('bqkgd,bskd->bkgqs', qblk, kb).astype(jnp.float32) * scale
        s_band = jnp.where(valid, s_band, NEG)
        s_ctx = jnp.einsum('bqkgd,bpkd->bkgqp', qblk, k_ctx).astype(jnp.float32) * scale
        s = jnp.concatenate([s_ctx, s_band], axis=-1)
        vals = jnp.concatenate([v_ctx, vb], axis=1)
        return sink_softmax_pv(s, vals, sink)

    out = lax.map(one, (jnp.arange(nb), qb))
    return out.swapaxes(0, 1).reshape(b, t_len, ATTN_WIDTH)


def mixer_inputs(h, w_in, conv_w):
    b, t_len = h.shape[:2]
    proj = h @ w_in
    offs = np.cumsum([ATTN_WIDTH, KV_WIDTH, KV_WIDTH, CONV_WIDTH, CONV_WIDTH]).tolist()
    q, k, v, gb, gc, xc = jnp.split(proj, offs, axis=-1)
    conv_out = gb * dwconv3(gc * xc, conv_w)
    q = q.reshape(b, t_len, N_HEADS, HEAD_DIM)
    k = k.reshape(b, t_len, N_KV_HEADS, HEAD_DIM)
    v = v.reshape(b, t_len, N_KV_HEADS, HEAD_DIM)
    return q, k, v, conv_out


def mixer_merge(attn_out, conv_out, g_attn_out, g_conv_out, w_out):
    y = jnp.concatenate([rmsnorm(attn_out, g_attn_out), rmsnorm(conv_out, g_conv_out)], axis=-1)
    return y @ w_out


def conv_ffn(h, w_up, ffn_conv_w, w_down):
    u = dwconv3(h @ w_up, ffn_conv_w)
    gate, val = jnp.split(u, 2, axis=-1)
    return (jax.nn.silu(gate) * val) @ w_down


def setup_inputs(seed: int = 0) -> dict:
    key = jax.random.key(seed)
    ks = jax.random.split(key, 20)
    f32 = jnp.float32

    def nrm(k, shape, s):
        return jax.random.normal(k, shape, f32) * s

    return {
        'x_prompt': nrm(ks[0], (BATCH, SEQ, D_MODEL), 1.0),
        'x_sample': nrm(ks[1], (DEC_BATCH, DEC_SEQ, D_MODEL), 1.0),
        'cache_k': nrm(ks[2], (DEC_BATCH, DEPTH, PAST_LEN, N_KV_HEADS, HEAD_DIM), 1.0),
        'cache_v': nrm(ks[3], (DEC_BATCH, DEPTH, PAST_LEN, N_KV_HEADS, HEAD_DIM), 1.0),
        'c': nrm(ks[4], (DEC_BATCH, D_MODEL), 1.0),
        'c_ctx': nrm(ks[5], (D_MODEL,), 1.0),
        'g_mix': 1.0 + nrm(ks[6], (DEPTH, D_MODEL), 0.02),
        'w_mod': nrm(ks[7], (DEPTH, D_MODEL, 6 * D_MODEL), 0.5 * D_MODEL ** -0.5),
        'b_mod': nrm(ks[8], (DEPTH, 6 * D_MODEL), 0.02),
        'w_in': nrm(ks[9], (DEPTH, D_MODEL, IN_WIDTH), D_MODEL ** -0.5),
        'conv_w': nrm(ks[10], (DEPTH, CONV_K, CONV_WIDTH), CONV_K ** -0.5),
        'g_attn_out': 1.0 + nrm(ks[11], (DEPTH, ATTN_WIDTH), 0.02),
        'g_conv_out': 1.0 + nrm(ks[12], (DEPTH, CONV_WIDTH), 0.02),
        'attn_sink': nrm(ks[13], (DEPTH, N_HEADS), 0.5),
        'w_out': nrm(ks[14], (DEPTH, D_MODEL, D_MODEL), D_MODEL ** -0.5),
        'g_ffn': 1.0 + nrm(ks[15], (DEPTH, D_MODEL), 0.02),
        'w_up': nrm(ks[16], (DEPTH, D_MODEL, 2 * D_FF), D_MODEL ** -0.5),
        'ffn_conv_w': nrm(ks[17], (DEPTH, CONV_K, 2 * D_FF), CONV_K ** -0.5),
        'w_down': nrm(ks[18], (DEPTH, D_FF, D_MODEL), D_FF ** -0.5),
        'g_final': 1.0 + nrm(ks[19], (D_MODEL,), 0.02),
    }


def reference(x_prompt, x_sample, cache_k, cache_v, c, c_ctx, g_mix, w_mod, b_mod, w_in,
              conv_w, g_attn_out, g_conv_out, attn_sink, w_out, g_ffn, w_up, ffn_conv_w,
              w_down, g_final):
    xp = x_prompt
    xs = x_sample
    cos, sin = axial_rope_tables(x_sample.shape[1])
    new_k, new_v = [], []
    for l in range(DEPTH):
        sh1, sc1, gt1, sh2, sc2, gt2 = adaln_params(c_ctx[None, :], w_mod[l], b_mod[l])
        h = modulate(rmsnorm(xp, g_mix[l]), sh1, sc1)
        q, k, v, conv_out = mixer_inputs(h, w_in[l], conv_w[l])
        attn = context_attention(q, k, v, attn_sink[l])
        xp = xp + gt1 * mixer_merge(attn, conv_out, g_attn_out[l], g_conv_out[l], w_out[l])
        h = modulate(rmsnorm(xp, g_ffn[l]), sh2, sc2)
        xp = xp + gt2 * conv_ffn(h, w_up[l], ffn_conv_w[l], w_down[l])
        new_k.append(k)
        new_v.append(v)

        sh1, sc1, gt1, sh2, sc2, gt2 = adaln_params(c, w_mod[l], b_mod[l])
        h = modulate(rmsnorm(xs, g_mix[l]), sh1, sc1)
        q, k, v, conv_out = mixer_inputs(h, w_in[l], conv_w[l])
        q = apply_axial_rope(q, cos, sin)
        k = apply_axial_rope(k, cos, sin)
        attn = latent_attention(q, k, v, cache_k[:, l], cache_v[:, l], attn_sink[l])
        xs = xs + gt1 * mixer_merge(attn, conv_out, g_attn_out[l], g_conv_out[l], w_out[l])
        h = modulate(rmsnorm(xs, g_ffn[l]), sh2, sc2)
        xs = xs + gt2 * conv_ffn(h, w_up[l], ffn_conv_w[l], w_down[l])

    y_prompt = rmsnorm(xp, g_final)
    y_sample = rmsnorm(xs, g_final)
    new_cache_k = jnp.stack(new_k, axis=1)
    new_cache_v = jnp.stack(new_v, axis=1)
    return (y_prompt, y_sample, new_cache_k, new_cache_v)
```

```python
import functools
import math

import jax
import jax.numpy as jnp
from jax import lax
from jax.experimental import pallas as pl
from jax.experimental.pallas import tpu as pltpu

HEAD_DIM = 128
GRID_W = 64
WINDOW = 128
Q_BLOCK = 128
ROPE_THETA = 10000.0
EPS = 1e-6
NEG = -1e30

F32 = jnp.float32
BF16 = jnp.bfloat16

V7X_VMEM_BYTES = 64 * 2**20
VMEM_LIMIT_BYTES = V7X_VMEM_BYTES - 8 * 2**20
LANE = 128
BF16_SUBLANES = 16
MOD_ROWS = 8


def _pick(dim, pref, mult):
    t = (min(pref, dim) // mult) * mult
    while t > mult and dim % t:
        t -= mult
    assert t > 0 and dim % t == 0, (dim, pref, mult)
    return t


def _params(semantics):
    return pltpu.CompilerParams(dimension_semantics=semantics, vmem_limit_bytes=VMEM_LIMIT_BYTES)


def _silu(x):
    return x * (1.0 / (1.0 + jnp.exp(-x)))


class _Rows:
    def __init__(self, n_ctx_seq, ctx_len, n_lat_seq, lat_len):
        self.ctx_len, self.lat_len = ctx_len, lat_len
        self.n_ctx_seq, self.n_lat_seq = n_ctx_seq, n_lat_seq
        self.m_ctx = n_ctx_seq * ctx_len
        self.m_lat = n_lat_seq * lat_len
        self.m = self.m_ctx + self.m_lat

    def mod_row(self, i, tm):
        nc = self.m_ctx // tm
        return jnp.where(i < nc, 0, 1 + (i - nc) // (self.lat_len // tm))

    def lat_block(self, i, tm):
        nc = self.m_ctx // tm
        return jnp.maximum(i - nc, 0) % (self.lat_len // tm)


def _adaln_kernel(c_ref, w_ref, b_ref, o_ref):
    s = _silu(c_ref[...]).astype(BF16)
    o_ref[...] = jnp.dot(s, w_ref[...].astype(BF16), preferred_element_type=F32) + b_ref[...]


def _adaln(cvec, w_mod, b_mod):
    depth, d, n = w_mod.shape
    tn = _pick(n, 512, LANE)
    return pl.pallas_call(
        _adaln_kernel,
        grid=(depth, n // tn),
        in_specs=[
            pl.BlockSpec((MOD_ROWS, d), lambda l, j: (0, 0)),
            pl.BlockSpec((None, d, tn), lambda l, j: (l, 0, j)),
            pl.BlockSpec((None, 1, tn), lambda l, j: (l, 0, j)),
        ],
        out_specs=pl.BlockSpec((None, MOD_ROWS, tn), lambda l, j: (l, 0, j)),
        out_shape=jax.ShapeDtypeStruct((depth, MOD_ROWS, n), F32),
        compiler_params=_params(("arbitrary", "arbitrary")),
        name="adaln",
    )(cvec, w_mod, b_mod.reshape(depth, 1, n))


def _rms(x):
    return x * lax.rsqrt(jnp.mean(x * x, axis=-1, keepdims=True) + EPS)


def _norm_mod_kernel(x_ref, g_ref, sc_ref, sh_ref, o_ref):
    y = _rms(x_ref[...]) * g_ref[...]
    o_ref[...] = (y * (1.0 + sc_ref[...]) + sh_ref[...]).astype(o_ref.dtype)


def _norm_mod(x, g, mod, layer, sc_idx, sh_idx, rows):
    m, d = x.shape
    tr = _pick(math.gcd(rows.m_ctx, rows.lat_len), 256, 8)

    def mod_spec(c):
        return pl.BlockSpec((None, None, None, 1, d), lambda i: (layer, rows.mod_row(i, tr), c, 0, 0))

    return pl.pallas_call(
        _norm_mod_kernel,
        grid=(m // tr,),
        in_specs=[
            pl.BlockSpec((tr, d), lambda i: (i, 0)),
            pl.BlockSpec((None, 1, d), lambda i: (layer, 0, 0)),
            mod_spec(sc_idx),
            mod_spec(sh_idx),
        ],
        out_specs=pl.BlockSpec((tr, d), lambda i: (i, 0)),
        out_shape=jax.ShapeDtypeStruct((m, d), BF16),
        compiler_params=_params(("arbitrary",)),
        name="norm_mod",
    )(x, g.reshape(g.shape[0], 1, d), mod, mod)


def _final_norm_kernel(x_ref, g_ref, o_ref):
    o_ref[...] = _rms(x_ref[...]) * g_ref[...]


def _final_norm(x, g):
    m, d = x.shape
    tr = _pick(m, 256, 8)
    return pl.pallas_call(
        _final_norm_kernel,
        grid=(m // tr,),
        in_specs=[pl.BlockSpec((tr, d), lambda i: (i, 0)), pl.BlockSpec((1, d), lambda i: (0, 0))],
        out_specs=pl.BlockSpec((tr, d), lambda i: (i, 0)),
        out_shape=jax.ShapeDtypeStruct((m, d), F32),
        compiler_params=_params(("arbitrary",)),
        name="final_norm",
    )(x, g.reshape(1, d))


def _merge_norm_kernel(a_ref, c_ref, ga_ref, gc_ref, o_ref):
    wa = a_ref.shape[-1]
    o_ref[:, :wa] = (_rms(a_ref[...].astype(F32)) * ga_ref[...]).astype(o_ref.dtype)
    o_ref[:, wa:] = (_rms(c_ref[...].astype(F32)) * gc_ref[...]).astype(o_ref.dtype)


def _merge_norm(attn, conv, g_attn, g_conv, layer):
    m, wa = attn.shape
    wc = conv.shape[1]
    tr = _pick(m, 256, BF16_SUBLANES)
    return pl.pallas_call(
        _merge_norm_kernel,
        grid=(m // tr,),
        in_specs=[
            pl.BlockSpec((tr, wa), lambda i: (i, 0)),
            pl.BlockSpec((tr, wc), lambda i: (i, 0)),
            pl.BlockSpec((None, 1, wa), lambda i: (layer, 0, 0)),
            pl.BlockSpec((None, 1, wc), lambda i: (layer, 0, 0)),
        ],
        out_specs=pl.BlockSpec((tr, wa + wc), lambda i: (i, 0)),
        out_shape=jax.ShapeDtypeStruct((m, wa + wc), BF16),
        compiler_params=_params(("arbitrary",)),
        name="merge_norm",
    )(attn, conv, g_attn.reshape(-1, 1, wa), g_conv.reshape(-1, 1, wc))


def _rope_tables(t_len):
    pos = jnp.arange(t_len)
    row = (pos // GRID_W).astype(F32)
    col = (pos % GRID_W).astype(F32)
    n_freq = HEAD_DIM // 4
    inv = ROPE_THETA ** (-jnp.arange(n_freq, dtype=F32) / n_freq)
    ar = row[:, None] * inv
    ac = col[:, None] * inv
    ang = jnp.concatenate([ar, ar, ac, ac], axis=-1)
    cos, sin = jnp.cos(ang), jnp.sin(ang)
    lower = (jnp.arange(HEAD_DIM) % (2 * n_freq)) < n_freq
    return jnp.concatenate([cos, jnp.where(lower, -sin, 0.0), jnp.where(lower, 0.0, sin)], axis=-1)


def _qkv_kernel(h_ref, w_ref, tab_ref, o_ref, *, n_ctx_blocks, rope_heads):
    i, j = pl.program_id(0), pl.program_id(1)
    acc = jnp.dot(h_ref[...], w_ref[...], preferred_element_type=F32)
    heads = o_ref.shape[-1] // HEAD_DIM

    def emit(n_rope):
        for hc in range(heads):
            a = acc[:, hc * HEAD_DIM:(hc + 1) * HEAD_DIM]
            if hc < n_rope:
                cos = tab_ref[:, :HEAD_DIM]
                sin_up = tab_ref[:, HEAD_DIM:2 * HEAD_DIM]
                sin_dn = tab_ref[:, 2 * HEAD_DIM:]
                a = (a * cos + pltpu.roll(a, HEAD_DIM - HEAD_DIM // 4, 1) * sin_up
                     + pltpu.roll(a, HEAD_DIM // 4, 1) * sin_dn)
            o_ref[:, hc * HEAD_DIM:(hc + 1) * HEAD_DIM] = a.astype(o_ref.dtype)

    is_lat = i >= n_ctx_blocks
    for jj, n_rope in enumerate(rope_heads):
        pl.when(jnp.logical_and(is_lat, j == jj))(functools.partial(emit, n_rope))
    pl.when(jnp.logical_not(is_lat))(functools.partial(emit, 0))


def _qkv_proj(h, w_in, layer, n_cols, n_rope_cols, tab, rows):
    m, d = h.shape
    tm = _pick(math.gcd(rows.m_ctx, rows.lat_len), 1024, BF16_SUBLANES)
    tn = _pick(n_cols, 1024, HEAD_DIM)
    n_tiles = n_cols // tn
    rope_heads = tuple(min(max(n_rope_cols - jj * tn, 0), tn) // HEAD_DIM for jj in range(n_tiles))
    kern = functools.partial(_qkv_kernel, n_ctx_blocks=rows.m_ctx // tm, rope_heads=rope_heads)
    return pl.pallas_call(
        kern,
        grid=(m // tm, n_tiles),
        in_specs=[
            pl.BlockSpec((tm, d), lambda i, j: (i, 0)),
            pl.BlockSpec((None, d, tn), lambda i, j: (layer, 0, j)),
            pl.BlockSpec((tm, 3 * HEAD_DIM), lambda i, j: (rows.lat_block(i, tm), 0)),
        ],
        out_specs=pl.BlockSpec((tm, tn), lambda i, j: (i, j)),
        out_shape=jax.ShapeDtypeStruct((m, n_cols), BF16),
        compiler_params=_params(("arbitrary", "arbitrary")),
        name="qkv_proj",
    )(h, w_in, tab)


def _gconv_kernel(hm_ref, hp_ref, hn_ref, *refs, mode, n_w, rows, tm):
    w_refs = refs[:n_w]
    cw_refs = refs[n_w:-2]
    o_ref, hx_ref = refs[-2], refs[-1]
    i, j = pl.program_id(0), pl.program_id(1)
    halo = BF16_SUBLANES

    @pl.when(j == 0)
    def _():
        hx_ref[:halo] = hp_ref[...]
        hx_ref[halo:halo + tm] = hm_ref[...]
        hx_ref[halo + tm:] = hn_ref[...]

    r = i * tm + lax.broadcasted_iota(jnp.int32, (tm, 1), 0)
    is_lat = i >= rows.m_ctx // tm
    pos = jnp.where(is_lat, (r - rows.m_ctx) % rows.lat_len, r % rows.ctx_len)
    last = jnp.where(is_lat, rows.lat_len - 1, rows.ctx_len - 1)
    has_prev = pos != 0
    has_next = pos != last
    n_ext = tm + 2 * halo

    def conv(u, cw_ref):
        prev = pltpu.roll(u, 1, 0)[halo:halo + tm]
        nxt = pltpu.roll(u, n_ext - 1, 0)[halo:halo + tm]
        mid = u[halo:halo + tm]
        return (cw_ref[0:1, :] * jnp.where(has_prev, prev, 0.0) + cw_ref[1:2, :] * mid
                + cw_ref[2:3, :] * jnp.where(has_next, nxt, 0.0))

    hx = hx_ref[...]
    us = [jnp.dot(hx, w_ref[...], preferred_element_type=F32) for w_ref in w_refs]
    if mode == "mixer":
        u_gate, u_a, u_b = us
        out = u_gate[halo:halo + tm] * conv(u_a * u_b, cw_refs[0])
    else:
        out = _silu(conv(us[0], cw_refs[0])) * conv(us[1], cw_refs[1])
    o_ref[...] = out.astype(o_ref.dtype)


def _gconv_proj(h, w, layer, w_col0, conv_w, cw_col0, width, mode, rows):
    m, d = h.shape
    n_w = 3 if mode == "mixer" else 2
    n_cw = 1 if mode == "mixer" else 2
    halo = BF16_SUBLANES
    tm = _pick(math.gcd(rows.m_ctx, rows.lat_len), 1024, halo)
    tf = _pick(math.gcd(width, math.gcd(w_col0, cw_col0) or width), 256, LANE)
    hb = tm // halo
    n_hb = m // halo

    def w_spec(g):
        off = (w_col0 + g * width) // tf
        return pl.BlockSpec((None, d, tf), lambda i, j: (layer, 0, off + j))

    def cw_spec(g):
        off = (cw_col0 + g * width) // tf
        return pl.BlockSpec((None, 3, tf), lambda i, j: (layer, 0, off + j))

    kern = functools.partial(_gconv_kernel, mode=mode, n_w=n_w, rows=rows, tm=tm)
    return pl.pallas_call(
        kern,
        grid=(m // tm, width // tf),
        in_specs=[
            pl.BlockSpec((tm, d), lambda i, j: (i, 0)),
            pl.BlockSpec((halo, d), lambda i, j: (jnp.maximum(i * hb - 1, 0), 0)),
            pl.BlockSpec((halo, d), lambda i, j: (jnp.minimum((i + 1) * hb, n_hb - 1), 0)),
        ] + [w_spec(g) for g in range(n_w)] + [cw_spec(g) for g in range(n_cw)],
        out_specs=pl.BlockSpec((tm, tf), lambda i, j: (i, j)),
        out_shape=jax.ShapeDtypeStruct((m, width), BF16),
        scratch_shapes=[pltpu.VMEM((tm + 2 * halo, d), BF16)],
        compiler_params=_params(("arbitrary", "arbitrary")),
        name="gconv_" + mode,
    )(h, h, h, *([w] * n_w), *([conv_w] * n_cw))


def _resid_kernel(a_ref, w_ref, x_ref, gt_ref, o_ref):
    p = jnp.dot(a_ref[...], w_ref[...], preferred_element_type=F32)
    o_ref[...] = x_ref[...] + gt_ref[...] * p


def _resid_proj(a, w, layer, x, mod, gt_idx, rows, tn_pref):
    m, k = a.shape
    n = w.shape[-1]
    tm = _pick(math.gcd(rows.m_ctx, rows.lat_len), 1024, BF16_SUBLANES)
    tn = _pick(n, tn_pref, LANE)
    return pl.pallas_call(
        _resid_kernel,
        grid=(m // tm, n // tn),
        in_specs=[
            pl.BlockSpec((tm, k), lambda i, j: (i, 0), pipeline_mode=pl.Buffered(1)),
            pl.BlockSpec((None, k, tn), lambda i, j: (layer, 0, j)),
            pl.BlockSpec((tm, tn), lambda i, j: (i, j)),
            pl.BlockSpec((None, None, None, 1, tn),
                         lambda i, j: (layer, rows.mod_row(i, tm), gt_idx, 0, j)),
        ],
        out_specs=pl.BlockSpec((tm, tn), lambda i, j: (i, j)),
        out_shape=jax.ShapeDtypeStruct((m, n), F32),
        compiler_params=_params(("arbitrary", "arbitrary")),
        name="resid_proj",
    )(a, w, x, mod)


def _stack_heads(q_ref, kv_head, group):
    base = kv_head * group
    return jnp.concatenate(
        [q_ref[:, (base + g) * HEAD_DIM:(base + g + 1) * HEAD_DIM] for g in range(group)], axis=0)


def _sink_column(sink_ref, kv_head, group, rows_per_head):
    rg = lax.broadcasted_iota(jnp.int32, (group * rows_per_head, 1), 0) // rows_per_head
    sk = jnp.zeros((group * rows_per_head, 1), F32)
    for g in range(group):
        sk = jnp.where(rg == g, sink_ref[kv_head * group + g], sk)
    return sk


def _qk(q, k):
    return lax.dot_general(q, k, (((1,), (1,)), ((), ())), preferred_element_type=F32)


def _ctx_attn_kernel(sink_ref, q_ref, k_ref, v_ref, o_ref, *, group, n_kv, scale):
    s_len = q_ref.shape[0]
    for h in range(n_kv):
        q = _stack_heads(q_ref, h, group)
        k = k_ref[:, h * HEAD_DIM:(h + 1) * HEAD_DIM]
        v = v_ref[:, h * HEAD_DIM:(h + 1) * HEAD_DIM]
        s = _qk(q, k) * scale
        sk = _sink_column(sink_ref, h, group, s_len)
        mx = jnp.maximum(jnp.max(s, axis=-1, keepdims=True), sk)
        e = jnp.exp(s - mx)
        inv = 1.0 / (jnp.sum(e, axis=-1, keepdims=True) + jnp.exp(sk - mx))
        o = jnp.dot((e * inv).astype(BF16), v, preferred_element_type=F32)
        for g in range(group):
            c0 = (h * group + g) * HEAD_DIM
            o_ref[:, c0:c0 + HEAD_DIM] = o[g * s_len:(g + 1) * s_len].astype(o_ref.dtype)


def _ctx_attention(qkv, sink, rows, n_heads, n_kv):
    m = qkv.shape[0]
    attn_w, kv_w = n_heads * HEAD_DIM, n_kv * HEAD_DIM
    s_len = rows.ctx_len
    kern = functools.partial(_ctx_attn_kernel, group=n_heads // n_kv, n_kv=n_kv, scale=HEAD_DIM ** -0.5)
    return pl.pallas_call(
        kern,
        grid=(rows.n_ctx_seq,),
        in_specs=[
            pl.BlockSpec(memory_space=pltpu.SMEM),
            pl.BlockSpec((s_len, attn_w), lambda b: (b, 0)),
            pl.BlockSpec((s_len, kv_w), lambda b: (b, attn_w // kv_w)),
            pl.BlockSpec((s_len, kv_w), lambda b: (b, attn_w // kv_w + 1)),
        ],
        out_specs=pl.BlockSpec((s_len, attn_w), lambda b: (b, 0)),
        out_shape=jax.ShapeDtypeStruct((m, attn_w), BF16),
        compiler_params=_params(("arbitrary",)),
        name="ctx_attention",
    )(sink, qkv, qkv, qkv)


def _lat_attn_kernel(sink_ref, q_ref, k_ref, v_ref, kc_ref, vc_ref, prev_ref, o_ref, *,
                     group, n_kv, t_len, scale):
    del prev_ref
    blk = pl.program_id(1)
    band = 3 * Q_BLOCK
    start = pl.multiple_of(jnp.clip((blk - 1) * Q_BLOCK, 0, t_len - band), Q_BLOCK)
    kb = k_ref[pl.ds(start, band), :]
    vb = v_ref[pl.ds(start, band), :]
    n_rows = group * Q_BLOCK
    qpos = blk * Q_BLOCK + lax.broadcasted_iota(jnp.int32, (n_rows, band), 0) % Q_BLOCK
    kpos = start + lax.broadcasted_iota(jnp.int32, (n_rows, band), 1)
    valid = jnp.abs(kpos - qpos) <= WINDOW
    for h in range(n_kv):
        cols = slice(h * HEAD_DIM, (h + 1) * HEAD_DIM)
        q = _stack_heads(q_ref, h, group)
        s_c = _qk(q, kc_ref[:, cols]) * scale
        s_b = jnp.where(valid, _qk(q, kb[:, cols]) * scale, NEG)
        sk = _sink_column(sink_ref, h, group, Q_BLOCK)
        mx = jnp.maximum(jnp.maximum(jnp.max(s_c, axis=-1, keepdims=True),
                                     jnp.max(s_b, axis=-1, keepdims=True)), sk)
        e_c = jnp.exp(s_c - mx)
        e_b = jnp.exp(s_b - mx)
        inv = 1.0 / (jnp.sum(e_c, axis=-1, keepdims=True) + jnp.sum(e_b, axis=-1, keepdims=True)
                     + jnp.exp(sk - mx))
        o = (jnp.dot((e_c * inv).astype(BF16), vc_ref[:, cols], preferred_element_type=F32)
             + jnp.dot((e_b * inv).astype(BF16), vb[:, cols], preferred_element_type=F32))
        for g in range(group):
            c0 = (h * group + g) * HEAD_DIM
            o_ref[:, c0:c0 + HEAD_DIM] = o[g * Q_BLOCK:(g + 1) * Q_BLOCK].astype(o_ref.dtype)


def _lat_attention(qkv, cache_k, cache_v, layer, sink, attn_prev, rows, n_heads, n_kv):
    attn_w, kv_w = n_heads * HEAD_DIM, n_kv * HEAD_DIM
    t_len = rows.lat_len
    p_len = cache_k.shape[2]
    assert rows.m_ctx % t_len == 0 and t_len >= 3 * Q_BLOCK and t_len % Q_BLOCK == 0
    nblk = t_len // Q_BLOCK
    q0 = rows.m_ctx // Q_BLOCK
    s0 = rows.m_ctx // t_len
    kern = functools.partial(_lat_attn_kernel, group=n_heads // n_kv, n_kv=n_kv, t_len=t_len,
                             scale=HEAD_DIM ** -0.5)
    return pl.pallas_call(
        kern,
        grid=(rows.n_lat_seq, nblk),
        in_specs=[
            pl.BlockSpec(memory_space=pltpu.SMEM),
            pl.BlockSpec((Q_BLOCK, attn_w), lambda b, q: (q0 + b * nblk + q, 0)),
            pl.BlockSpec((t_len, kv_w), lambda b, q: (s0 + b, attn_w // kv_w)),
            pl.BlockSpec((t_len, kv_w), lambda b, q: (s0 + b, attn_w // kv_w + 1)),
            pl.BlockSpec((None, None, p_len, kv_w), lambda b, q: (b, layer, 0, 0)),
            pl.BlockSpec((None, None, p_len, kv_w), lambda b, q: (b, layer, 0, 0)),
            pl.BlockSpec(memory_space=pl.ANY),
        ],
        out_specs=pl.BlockSpec((Q_BLOCK, attn_w), lambda b, q: (q0 + b * nblk + q, 0)),
        out_shape=jax.ShapeDtypeStruct(attn_prev.shape, attn_prev.dtype),
        input_output_aliases={6: 0},
        compiler_params=_params(("arbitrary", "arbitrary")),
        name="lat_attention",
    )(sink, qkv, qkv, qkv, cache_k, cache_v, attn_prev)


def kernel(x_prompt, x_sample, cache_k, cache_v, c, c_ctx, g_mix, w_mod, b_mod, w_in, conv_w,
           g_attn_out, g_conv_out, attn_sink, w_out, g_ffn, w_up, ffn_conv_w, w_down, g_final):
    batch, seq, d = x_prompt.shape
    dec_batch, dec_seq, _ = x_sample.shape
    depth = w_in.shape[0]
    n_heads = attn_sink.shape[1]
    n_kv = cache_k.shape[3]
    attn_w, kv_w = n_heads * HEAD_DIM, n_kv * HEAD_DIM
    conv_width = d - attn_w
    d_ff = w_down.shape[1]
    qkv_w = attn_w + 2 * kv_w
    assert cache_k.shape[0] == dec_batch and cache_k.shape[4] == HEAD_DIM
    assert w_in.shape[2] == qkv_w + 3 * conv_width and w_up.shape[2] == 2 * d_ff
    assert dec_batch + 1 <= MOD_ROWS
    rows = _Rows(batch, seq, dec_batch, dec_seq)

    w_in_b, w_out_b, w_up_b, w_down_b = (w.astype(BF16) for w in (w_in, w_out, w_up, w_down))
    p_len = cache_k.shape[2]
    cache_k_b = cache_k.astype(BF16).reshape(dec_batch, depth, p_len, kv_w)
    cache_v_b = cache_v.astype(BF16).reshape(dec_batch, depth, p_len, kv_w)

    cvec = jnp.concatenate([c_ctx[None, :], c, jnp.zeros((MOD_ROWS - 1 - dec_batch, d), F32)], axis=0)
    mod = _adaln(cvec, w_mod, b_mod).reshape(depth, MOD_ROWS, 6, 1, d)
    tab = _rope_tables(dec_seq)

    x = jnp.concatenate([x_prompt.reshape(rows.m_ctx, d), x_sample.reshape(rows.m_lat, d)], axis=0)
    new_k, new_v = [], []
    for l in range(depth):
        h = _norm_mod(x, g_mix, mod, l, 1, 0, rows)
        qkv = _qkv_proj(h, w_in_b, l, qkv_w, attn_w + kv_w, tab, rows)
        conv = _gconv_proj(h, w_in_b, l, qkv_w, conv_w, 0, conv_width, "mixer", rows)
        attn = _ctx_attention(qkv, attn_sink[l], rows, n_heads, n_kv)
        attn = _lat_attention(qkv, cache_k_b, cache_v_b, l, attn_sink[l], attn, rows, n_heads, n_kv)
        y = _merge_norm(attn, conv, g_attn_out, g_conv_out, l)
        x = _resid_proj(y, w_out_b, l, x, mod, 2, rows, 1024)
        h = _norm_mod(x, g_ffn, mod, l, 4, 3, rows)
        act = _gconv_proj(h, w_up_b, l, 0, ffn_conv_w, 0, d_ff, "ffn", rows)
        x = _resid_proj(act, w_down_b, l, x, mod, 5, rows, 256)
        ctx_kv = qkv[:rows.m_ctx, attn_w:].astype(F32)
        new_k.append(ctx_kv[:, :kv_w].reshape(batch, seq, n_kv, HEAD_DIM))
        new_v.append(ctx_kv[:, kv_w:].reshape(batch, seq, n_kv, HEAD_DIM))

    y = _final_norm(x, g_final)
    y_prompt = y[:rows.m_ctx].reshape(batch, seq, d)
    y_sample = y[rows.m_ctx:].reshape(dec_batch, dec_seq, d)
    return (y_prompt, y_sample, jnp.stack(new_k, axis=1), jnp.stack(new_v, axis=1))
```

```python
import functools
import math

import jax
import jax.numpy as jnp
from jax import lax
from jax.experimental import pallas as pl
from jax.experimental.pallas import tpu as pltpu

HEAD_DIM = 128
GRID_W = 64
WINDOW = 128
Q_BLOCK = 128
ROPE_THETA = 10000.0
EPS = 1e-6
NEG = -1e30

F32 = jnp.float32
BF16 = jnp.bfloat16

V7X_VMEM_BYTES = 64 * 2**20
VMEM_LIMIT_BYTES = V7X_VMEM_BYTES - 8 * 2**20
LANE = 128
F32_SUBLANES = 8
BF16_SUBLANES = 16
MOD_ROWS = F32_SUBLANES
U_PAD = F32_SUBLANES


def _pick(dim, pref, mult):
    t = (min(pref, dim) // mult) * mult
    while t > mult and dim % t:
        t -= mult
    assert t > 0 and dim % t == 0, (dim, pref, mult)
    return t


def _params(semantics):
    return pltpu.CompilerParams(dimension_semantics=semantics, vmem_limit_bytes=VMEM_LIMIT_BYTES)


def _silu(x):
    return x * (1.0 / (1.0 + jnp.exp(-x)))


class _Rows:
    def __init__(self, n_ctx_seq, ctx_len, n_lat_seq, lat_len):
        self.ctx_len, self.lat_len = ctx_len, lat_len
        self.n_ctx_seq, self.n_lat_seq = n_ctx_seq, n_lat_seq
        self.m_ctx = n_ctx_seq * ctx_len
        self.m_lat = n_lat_seq * lat_len
        self.m = self.m_ctx + self.m_lat

    def row_tile(self):
        tm = _pick(math.gcd(self.m_ctx, self.lat_len), 1024, BF16_SUBLANES)
        assert self.lat_len % tm == 0 and self.m_ctx % tm == 0
        return tm

    def mod_row(self, i, tm):
        nc = self.m_ctx // tm
        return jnp.where(i < nc, 0, 1 + (i - nc) // (self.lat_len // tm))

    def lat_block(self, i, tm):
        nc = self.m_ctx // tm
        return jnp.maximum(i - nc, 0) % (self.lat_len // tm)


def _adaln_kernel(c_ref, w_ref, b_ref, o_ref):
    s = _silu(c_ref[...]).astype(BF16)
    o_ref[...] = jnp.dot(s, w_ref[...].astype(BF16), preferred_element_type=F32) + b_ref[...]


def _adaln(cvec, w_mod, b_mod):
    depth, d, n = w_mod.shape
    tn = _pick(n, 512, LANE)
    return pl.pallas_call(
        _adaln_kernel,
        grid=(depth, n // tn),
        in_specs=[
            pl.BlockSpec((MOD_ROWS, d), lambda l, j: (0, 0)),
            pl.BlockSpec((None, d, tn), lambda l, j: (l, 0, j)),
            pl.BlockSpec((None, 1, tn), lambda l, j: (l, 0, j)),
        ],
        out_specs=pl.BlockSpec((None, MOD_ROWS, tn), lambda l, j: (l, 0, j)),
        out_shape=jax.ShapeDtypeStruct((depth, MOD_ROWS, n), F32),
        compiler_params=_params(("arbitrary", "arbitrary")),
        name="adaln",
    )(cvec, w_mod, b_mod.reshape(depth, 1, n))


def _rms(x):
    return x * lax.rsqrt(jnp.mean(x * x, axis=-1, keepdims=True) + EPS)


def _norm_mod_kernel(x_ref, g_ref, sc_ref, sh_ref, o_ref):
    y = _rms(x_ref[...]) * g_ref[...]
    o_ref[...] = (y * (1.0 + sc_ref[...]) + sh_ref[...]).astype(o_ref.dtype)


def _norm_mod(x, g, mod, layer, sc_idx, sh_idx, rows):
    m, d = x.shape
    tr = _pick(math.gcd(rows.m_ctx, rows.lat_len), 256, F32_SUBLANES)

    def mod_spec(c):
        return pl.BlockSpec((None, None, None, 1, d), lambda i: (layer, rows.mod_row(i, tr), c, 0, 0))

    return pl.pallas_call(
        _norm_mod_kernel,
        grid=(m // tr,),
        in_specs=[
            pl.BlockSpec((tr, d), lambda i: (i, 0)),
            pl.BlockSpec((None, 1, d), lambda i: (layer, 0, 0)),
            mod_spec(sc_idx),
            mod_spec(sh_idx),
        ],
        out_specs=pl.BlockSpec((tr, d), lambda i: (i, 0)),
        out_shape=jax.ShapeDtypeStruct((m, d), BF16),
        compiler_params=_params(("arbitrary",)),
        name="norm_mod",
    )(x, g.reshape(g.shape[0], 1, d), mod, mod)


def _final_norm_kernel(x_ref, g_ref, oc_ref, ol_ref, *, n_ctx_blocks):
    y = _rms(x_ref[...]) * g_ref[...]
    i = pl.program_id(0)

    @pl.when(i < n_ctx_blocks)
    def _():
        oc_ref[...] = y

    @pl.when(i >= n_ctx_blocks)
    def _():
        ol_ref[...] = y


def _final_norm(x, g, rows):
    m, d = x.shape
    tr = _pick(math.gcd(rows.m_ctx, rows.m_lat), 256, F32_SUBLANES)
    nc = rows.m_ctx // tr
    return pl.pallas_call(
        functools.partial(_final_norm_kernel, n_ctx_blocks=nc),
        grid=(m // tr,),
        in_specs=[pl.BlockSpec((tr, d), lambda i: (i, 0)), pl.BlockSpec((1, d), lambda i: (0, 0))],
        out_specs=[pl.BlockSpec((tr, d), lambda i: (jnp.minimum(i, nc - 1), 0)),
                   pl.BlockSpec((tr, d), lambda i: (jnp.maximum(i - nc, 0), 0))],
        out_shape=[jax.ShapeDtypeStruct((rows.m_ctx, d), F32), jax.ShapeDtypeStruct((rows.m_lat, d), F32)],
        compiler_params=_params(("arbitrary",)),
        name="final_norm",
    )(x, g.reshape(1, d))


def _merge_norm_kernel(a_ref, c_ref, ga_ref, gc_ref, o_ref):
    wa = a_ref.shape[-1]
    o_ref[:, :wa] = (_rms(a_ref[...].astype(F32)) * ga_ref[...]).astype(o_ref.dtype)
    o_ref[:, wa:] = (_rms(c_ref[...].astype(F32)) * gc_ref[...]).astype(o_ref.dtype)


def _merge_norm(attn, conv, g_attn, g_conv, layer):
    m, wa = attn.shape
    wc = conv.shape[1]
    tr = _pick(m, 256, BF16_SUBLANES)
    return pl.pallas_call(
        _merge_norm_kernel,
        grid=(m // tr,),
        in_specs=[
            pl.BlockSpec((tr, wa), lambda i: (i, 0)),
            pl.BlockSpec((tr, wc), lambda i: (i, 0)),
            pl.BlockSpec((None, 1, wa), lambda i: (layer, 0, 0)),
            pl.BlockSpec((None, 1, wc), lambda i: (layer, 0, 0)),
        ],
        out_specs=pl.BlockSpec((tr, wa + wc), lambda i: (i, 0)),
        out_shape=jax.ShapeDtypeStruct((m, wa + wc), BF16),
        compiler_params=_params(("arbitrary",)),
        name="merge_norm",
    )(attn, conv, g_attn.reshape(-1, 1, wa), g_conv.reshape(-1, 1, wc))


def _rope_tables(t_len):
    pos = jnp.arange(t_len)
    row = (pos // GRID_W).astype(F32)
    col = (pos % GRID_W).astype(F32)
    n_freq = HEAD_DIM // 4
    inv = ROPE_THETA ** (-jnp.arange(n_freq, dtype=F32) / n_freq)
    ar = row[:, None] * inv
    ac = col[:, None] * inv
    ang = jnp.concatenate([ar, ar, ac, ac], axis=-1)
    cos, sin = jnp.cos(ang), jnp.sin(ang)
    lower = (jnp.arange(HEAD_DIM) % (2 * n_freq)) < n_freq
    return jnp.concatenate([cos, jnp.where(lower, -sin, 0.0), jnp.where(lower, 0.0, sin)], axis=-1)


def _qkv_kernel(h_ref, w_ref, tab_ref, o_ref, acc0_ref, acc1_ref, *, n_ctx_blocks, rope_heads):
    i, j = pl.program_id(0), pl.program_id(1)
    accs = (acc0_ref, acc1_ref)
    heads = o_ref.shape[-1] // HEAD_DIM
    n_tiles = len(rope_heads)

    def matmul(slot):
        accs[slot][...] = jnp.dot(h_ref[...], w_ref[...], preferred_element_type=F32)

    def emit(slot, n_rope):
        for hc in range(heads):
            cols = slice(hc * HEAD_DIM, (hc + 1) * HEAD_DIM)
            a = accs[slot][:, cols]
            if hc < n_rope:
                cos = tab_ref[:, :HEAD_DIM]
                sin_up = tab_ref[:, HEAD_DIM:2 * HEAD_DIM]
                sin_dn = tab_ref[:, 2 * HEAD_DIM:]
                a = (a * cos + pltpu.roll(a, HEAD_DIM - HEAD_DIM // 4, 1) * sin_up
                     + pltpu.roll(a, HEAD_DIM // 4, 1) * sin_dn)
            o_ref[:, cols] = a.astype(o_ref.dtype)

    def step(jj, lat):
        if jj < n_tiles:
            matmul(jj % 2)
        if jj > 0:
            emit((jj - 1) % 2, rope_heads[jj - 1] if lat else 0)

    is_lat = i >= n_ctx_blocks
    for jj in range(n_tiles + 1):
        pl.when(jnp.logical_and(is_lat, j == jj))(functools.partial(step, jj, True))
        pl.when(jnp.logical_and(jnp.logical_not(is_lat), j == jj))(functools.partial(step, jj, False))


def _qkv_proj(h, w_qkv, layer, n_rope_cols, tab, rows):
    m, d = h.shape
    n_cols = w_qkv.shape[-1]
    tm = rows.row_tile()
    tn = _pick(math.gcd(n_cols, n_rope_cols), 512, HEAD_DIM)
    n_tiles = n_cols // tn
    rope_heads = tuple(min(max(n_rope_cols - jj * tn, 0), tn) // HEAD_DIM for jj in range(n_tiles))
    kern = functools.partial(_qkv_kernel, n_ctx_blocks=rows.m_ctx // tm, rope_heads=rope_heads)
    return pl.pallas_call(
        kern,
        grid=(m // tm, n_tiles + 1),
        in_specs=[
            pl.BlockSpec((tm, d), lambda i, j: (i, 0)),
            pl.BlockSpec((None, d, tn), lambda i, j: (layer, 0, jnp.minimum(j, n_tiles - 1))),
            pl.BlockSpec((tm, 3 * HEAD_DIM), lambda i, j: (rows.lat_block(i, tm), 0)),
        ],
        out_specs=pl.BlockSpec((tm, tn), lambda i, j: (i, jnp.maximum(j - 1, 0))),
        out_shape=jax.ShapeDtypeStruct((m, n_cols), BF16),
        scratch_shapes=[pltpu.VMEM((tm, tn), F32), pltpu.VMEM((tm, tn), F32)],
        compiler_params=_params(("arbitrary", "arbitrary")),
        name="qkv_proj",
    )(h, w_qkv, tab)


def _conv_tile(width):
    return _pick(width, 256, LANE)


def _interleave(w, n_groups, tf):
    lead, total = w.shape[:-1], w.shape[-1]
    width = total // n_groups
    w = w.reshape(*lead, n_groups, width // tf, tf)
    return jnp.swapaxes(w, -3, -2).reshape(*lead, total)


def _edge_kernel(h_ref, w_ref, o_ref):
    o_ref[...] = jnp.dot(h_ref[...], w_ref[...], preferred_element_type=F32)


def _edge_proj(h, w, layer, tm):
    m, d = h.shape
    n_cols = w.shape[-1]
    nblk = m // tm
    idx = []
    for i in range(nblk):
        idx += [max(i * tm - 1, 0), min((i + 1) * tm, m - 1)]
    n_rows = -(-len(idx) // BF16_SUBLANES) * BF16_SUBLANES
    idx += [0] * (n_rows - len(idx))
    h_edge = jnp.take(h, jnp.asarray(idx, jnp.int32), axis=0)
    tn = _pick(n_cols, 512, LANE)
    u = pl.pallas_call(
        _edge_kernel,
        grid=(n_cols // tn,),
        in_specs=[
            pl.BlockSpec((n_rows, d), lambda j: (0, 0)),
            pl.BlockSpec((None, d, tn), lambda j: (layer, 0, j)),
        ],
        out_specs=pl.BlockSpec((n_rows, tn), lambda j: (0, j)),
        out_shape=jax.ShapeDtypeStruct((n_rows, n_cols), F32),
        compiler_params=_params(("arbitrary",)),
        name="edge_proj",
    )(h_edge, w)
    return jnp.pad(u[:2 * nblk].reshape(nblk, 2, n_cols), ((0, 0), (0, F32_SUBLANES - 2), (0, 0)))


def _place_edges(u_ref, ue_ref, i, lat, rows, tm):
    before, after = ue_ref[0:1, :], ue_ref[1:2, :]
    if lat:
        seq_row = ((i - rows.m_ctx // tm) * tm) % rows.lat_len
        before = jnp.where(seq_row != 0, before, 0.0)
        after = jnp.where(seq_row + tm != rows.lat_len, after, 0.0)
    u_ref[U_PAD - 1:U_PAD, :] = before
    u_ref[U_PAD + tm:U_PAD + tm + 1, :] = after


def _conv_rows(x, cw_ref, lat, row0, rows):
    n_ext = x.shape[0]
    rc = n_ext - 2 * U_PAD
    prev = pltpu.roll(x, 1, 0)[U_PAD:U_PAD + rc]
    nxt = pltpu.roll(x, n_ext - 1, 0)[U_PAD:U_PAD + rc]
    if not lat:
        pos = (row0 + lax.broadcasted_iota(jnp.int32, (rc, 1), 0)) % rows.ctx_len
        prev = jnp.where(pos != 0, prev, 0.0)
        nxt = jnp.where(pos != rows.ctx_len - 1, nxt, 0.0)
    return cw_ref[0:1, :] * prev + cw_ref[1:2, :] * x[U_PAD:U_PAD + rc] + cw_ref[2:3, :] * nxt


def _mixer_kernel(h_ref, w_ref, ue_ref, cw_ref, o_ref, u0_ref, u1_ref, *, n_tiles, rows, tm, tf, n_chunks):
    u_slots = (u0_ref, u1_ref)
    i, j = pl.program_id(0), pl.program_id(1)
    rc = tm // n_chunks

    def finish_chunk(slot, c, lat):
        lo = c * rc
        u = u_slots[slot][lo:lo + rc + 2 * U_PAD, :]
        gate = u[U_PAD:U_PAD + rc, :tf]
        y = _conv_rows(u[:, tf:2 * tf] * u[:, 2 * tf:], cw_ref, lat, i * tm + lo, rows)
        o_ref[lo:lo + rc, :] = (gate * y).astype(o_ref.dtype)

    def step(mm_slot, ep_slot, lat):
        if ep_slot is not None:
            _place_edges(u_slots[ep_slot], ue_ref, i, lat, rows, tm)
        if mm_slot is not None:
            u_slots[mm_slot][U_PAD:U_PAD + tm, :] = jnp.dot(h_ref[...], w_ref[...], preferred_element_type=F32)
        if ep_slot is not None:
            for c in range(n_chunks):
                finish_chunk(ep_slot, c, lat)

    is_lat = i >= rows.m_ctx // tm
    pl.when(j == 0)(functools.partial(step, 0, None, True))
    for lat in (False, True):
        kind = is_lat if lat else jnp.logical_not(is_lat)
        for par in (0, 1):
            mid = jnp.logical_and(jnp.logical_and(j > 0, j < n_tiles), j % 2 == par)
            pl.when(jnp.logical_and(kind, mid))(functools.partial(step, par, 1 - par, lat))
        pl.when(jnp.logical_and(kind, j == n_tiles))(
            functools.partial(step, None, (n_tiles - 1) % 2, lat))


def _mixer_proj(h, w_il, layer, conv_w, rows):
    m, d = h.shape
    n_w = 3
    width = conv_w.shape[-1]
    tf = _conv_tile(width)
    tm = rows.row_tile()
    n_chunks = min(16, tm // BF16_SUBLANES)
    n_tiles = width // tf
    u_edge = _edge_proj(h, w_il, layer, tm)

    def done(j):
        return jnp.maximum(j - 1, 0)

    kern = functools.partial(_mixer_kernel, n_tiles=n_tiles, rows=rows, tm=tm, tf=tf, n_chunks=n_chunks)
    slot = pltpu.VMEM((tm + 2 * U_PAD, n_w * tf), F32)
    return pl.pallas_call(
        kern,
        grid=(m // tm, n_tiles + 1),
        in_specs=[
            pl.BlockSpec((tm, d), lambda i, j: (i, 0)),
            pl.BlockSpec((None, d, n_w * tf), lambda i, j: (layer, 0, jnp.minimum(j, n_tiles - 1))),
            pl.BlockSpec((None, F32_SUBLANES, n_w * tf), lambda i, j: (i, 0, done(j))),
            pl.BlockSpec((None, 3, tf), lambda i, j: (layer, 0, done(j))),
        ],
        out_specs=pl.BlockSpec((tm, tf), lambda i, j: (i, done(j))),
        out_shape=jax.ShapeDtypeStruct((m, width), BF16),
        scratch_shapes=[slot, slot],
        compiler_params=_params(("arbitrary", "arbitrary")),
        name="mixer_proj",
    )(h, w_il, u_edge, conv_w)


def _resid_kernel(a_ref, w_ref, x_ref, gt_ref, o_ref):
    p = jnp.dot(a_ref[...], w_ref[...], preferred_element_type=F32)
    o_ref[...] = x_ref[...] + gt_ref[...] * p


def _resid_proj(a, w, layer, x, mod, gt_idx, rows):
    m, k = a.shape
    n = w.shape[-1]
    tm = rows.row_tile()
    tn = _pick(n, 1024, LANE)
    return pl.pallas_call(
        _resid_kernel,
        grid=(m // tm, n // tn),
        in_specs=[
            pl.BlockSpec((tm, k), lambda i, j: (i, 0), pipeline_mode=pl.Buffered(1)),
            pl.BlockSpec((None, k, tn), lambda i, j: (layer, 0, j)),
            pl.BlockSpec((tm, tn), lambda i, j: (i, j)),
            pl.BlockSpec((None, None, None, 1, tn),
                         lambda i, j: (layer, rows.mod_row(i, tm), gt_idx, 0, j)),
        ],
        out_specs=pl.BlockSpec((tm, tn), lambda i, j: (i, j)),
        out_shape=jax.ShapeDtypeStruct((m, n), F32),
        compiler_params=_params(("arbitrary", "arbitrary")),
        name="resid_proj",
    )(a, w, x, mod)


def _stack_heads(q_ref, kv_head, group):
    base = kv_head * group
    return jnp.concatenate(
        [q_ref[:, (base + g) * HEAD_DIM:(base + g + 1) * HEAD_DIM] for g in range(group)], axis=0)


def _sink_column(sink_ref, kv_head, group, rows_per_head):
    rg = lax.broadcasted_iota(jnp.int32, (group * rows_per_head, 1), 0) // rows_per_head
    sk = jnp.zeros((group * rows_per_head, 1), F32)
    for g in range(group):
        sk = jnp.where(rg == g, sink_ref[kv_head * group + g], sk)
    return sk


def _qk(q, k):
    return lax.dot_general(q, k, (((1,), (1,)), ((), ())), preferred_element_type=F32)


def _ctx_attn_kernel(sink_ref, q_ref, k_ref, v_ref, o_ref, *, group, n_kv, scale):
    s_len = q_ref.shape[0]
    for h in range(n_kv):
        q = _stack_heads(q_ref, h, group)
        k = k_ref[:, h * HEAD_DIM:(h + 1) * HEAD_DIM]
        v = v_ref[:, h * HEAD_DIM:(h + 1) * HEAD_DIM]
        s = _qk(q, k) * scale
        sk = _sink_column(sink_ref, h, group, s_len)
        mx = jnp.maximum(jnp.max(s, axis=-1, keepdims=True), sk)
        e = jnp.exp(s - mx)
        inv = 1.0 / (jnp.sum(e, axis=-1, keepdims=True) + jnp.exp(sk - mx))
        o = jnp.dot((e * inv).astype(BF16), v, preferred_element_type=F32)
        for g in range(group):
            c0 = (h * group + g) * HEAD_DIM
            o_ref[:, c0:c0 + HEAD_DIM] = o[g * s_len:(g + 1) * s_len].astype(o_ref.dtype)


def _ctx_attention(qkv, sink, rows, n_heads, n_kv):
    m = qkv.shape[0]
    attn_w, kv_w = n_heads * HEAD_DIM, n_kv * HEAD_DIM
    s_len = rows.ctx_len
    kern = functools.partial(_ctx_attn_kernel, group=n_heads // n_kv, n_kv=n_kv, scale=HEAD_DIM ** -0.5)
    return pl.pallas_call(
        kern,
        grid=(rows.n_ctx_seq,),
        in_specs=[
            pl.BlockSpec(memory_space=pltpu.SMEM),
            pl.BlockSpec((s_len, attn_w), lambda b: (b, 0)),
            pl.BlockSpec((s_len, kv_w), lambda b: (b, attn_w // kv_w)),
            pl.BlockSpec((s_len, kv_w), lambda b: (b, attn_w // kv_w + 1)),
        ],
        out_specs=pl.BlockSpec((s_len, attn_w), lambda b: (b, 0)),
        out_shape=jax.ShapeDtypeStruct((m, attn_w), BF16),
        compiler_params=_params(("arbitrary",)),
        name="ctx_attention",
    )(sink, qkv, qkv, qkv)


def _lat_attn_kernel(sink_ref, q_ref, k_ref, v_ref, kc_ref, vc_ref, prev_ref, o_ref, *,
                     group, n_kv, t_len, scale):
    del prev_ref
    blk = pl.program_id(1)
    band = 3 * Q_BLOCK
    start = pl.multiple_of(jnp.clip((blk - 1) * Q_BLOCK, 0, t_len - band), Q_BLOCK)
    kb = k_ref[pl.ds(start, band), :]
    vb = v_ref[pl.ds(start, band), :]
    n_rows = group * Q_BLOCK
    qpos = blk * Q_BLOCK + lax.broadcasted_iota(jnp.int32, (n_rows, band), 0) % Q_BLOCK
    kpos = start + lax.broadcasted_iota(jnp.int32, (n_rows, band), 1)
    valid = jnp.abs(kpos - qpos) <= WINDOW
    for h in range(n_kv):
        cols = slice(h * HEAD_DIM, (h + 1) * HEAD_DIM)
        q = _stack_heads(q_ref, h, group)
        s_c = _qk(q, kc_ref[:, cols]) * scale
        s_b = jnp.where(valid, _qk(q, kb[:, cols]) * scale, NEG)
        sk = _sink_column(sink_ref, h, group, Q_BLOCK)
        mx = jnp.maximum(jnp.maximum(jnp.max(s_c, axis=-1, keepdims=True),
                                     jnp.max(s_b, axis=-1, keepdims=True)), sk)
        e_c = jnp.exp(s_c - mx)
        e_b = jnp.exp(s_b - mx)
        inv = 1.0 / (jnp.sum(e_c, axis=-1, keepdims=True) + jnp.sum(e_b, axis=-1, keepdims=True)
                     + jnp.exp(sk - mx))
        o = (jnp.dot((e_c * inv).astype(BF16), vc_ref[:, cols], preferred_element_type=F32)
             + jnp.dot((e_b * inv).astype(BF16), vb[:, cols], preferred_element_type=F32))
        for g in range(group):
            c0 = (h * group + g) * HEAD_DIM
            o_ref[:, c0:c0 + HEAD_DIM] = o[g * Q_BLOCK:(g + 1) * Q_BLOCK].astype(o_ref.dtype)


def _lat_attention(qkv, cache_k, cache_v, layer, sink, attn_prev, rows, n_heads, n_kv):
    attn_w, kv_w = n_heads * HEAD_DIM, n_kv * HEAD_DIM
    t_len = rows.lat_len
    p_len = cache_k.shape[2]
    assert rows.m_ctx % t_len == 0 and t_len >= 3 * Q_BLOCK and t_len % Q_BLOCK == 0
    nblk = t_len // Q_BLOCK
    q0 = rows.m_ctx // Q_BLOCK
    s0 = rows.m_ctx // t_len
    kern = functools.partial(_lat_attn_kernel, group=n_heads // n_kv, n_kv=n_kv, t_len=t_len,
                             scale=HEAD_DIM ** -0.5)
    return pl.pallas_call(
        kern,
        grid=(rows.n_lat_seq, nblk),
        in_specs=[
            pl.BlockSpec(memory_space=pltpu.SMEM),
            pl.BlockSpec((Q_BLOCK, attn_w), lambda b, q: (q0 + b * nblk + q, 0)),
            pl.BlockSpec((t_len, kv_w), lambda b, q: (s0 + b, attn_w // kv_w)),
            pl.BlockSpec((t_len, kv_w), lambda b, q: (s0 + b, attn_w // kv_w + 1)),
            pl.BlockSpec((None, None, p_len, kv_w), lambda b, q: (b, layer, 0, 0)),
            pl.BlockSpec((None, None, p_len, kv_w), lambda b, q: (b, layer, 0, 0)),
            pl.BlockSpec(memory_space=pl.ANY),
        ],
        out_specs=pl.BlockSpec((Q_BLOCK, attn_w), lambda b, q: (q0 + b * nblk + q, 0)),
        out_shape=jax.ShapeDtypeStruct(attn_prev.shape, attn_prev.dtype),
        input_output_aliases={6: 0},
        compiler_params=_params(("arbitrary", "arbitrary")),
        name="lat_attention",
    )(sink, qkv, qkv, qkv, cache_k, cache_v, attn_prev)


def _ffn_kernel(h_ref, wu_ref, ue_ref, cw_ref, wd_ref, x_ref, gt_ref, o_ref, u0_ref, u1_ref, act_ref, acc_ref, *,
                n_tiles, rows, tm, tf, tq, n_chunks):
    u_slots = (u0_ref, u1_ref)
    i, j = pl.program_id(0), pl.program_id(1)
    rc = tm // n_chunks
    n_out = acc_ref.shape[0]

    def gate_chunk(slot, c, lat):
        lo = c * rc
        y = _conv_rows(u_slots[slot][lo:lo + rc + 2 * U_PAD, :], cw_ref, lat, i * tm + lo, rows)
        act_ref[lo:lo + rc, :] = (_silu(y[:, :tf]) * y[:, tf:]).astype(act_ref.dtype)

    def down_matmul(first):
        act = act_ref[...]
        for q in range(n_out):
            p = jnp.dot(act, wd_ref[:, q * tq:(q + 1) * tq], preferred_element_type=F32)
            if first:
                acc_ref[q] = p
            else:
                acc_ref[q] += p

    def step(up_slot, gate_slot, first, lat):
        if gate_slot is not None:
            _place_edges(u_slots[gate_slot], ue_ref, i, lat, rows, tm)
        if up_slot is not None:
            u_slots[up_slot][U_PAD:U_PAD + tm, :] = jnp.dot(h_ref[...], wu_ref[...], preferred_element_type=F32)
        if gate_slot is not None:
            for c in range(n_chunks):
                gate_chunk(gate_slot, c, lat)
            down_matmul(first)

    is_lat = i >= rows.m_ctx // tm
    pl.when(j == 0)(functools.partial(step, 0, None, False, True))
    for lat in (False, True):
        kind = is_lat if lat else jnp.logical_not(is_lat)
        pl.when(jnp.logical_and(kind, j == 1))(
            functools.partial(step, 1 if n_tiles > 1 else None, 0, True, lat))
        for par in (0, 1):
            mid = jnp.logical_and(jnp.logical_and(j > 1, j < n_tiles), j % 2 == par)
            pl.when(jnp.logical_and(kind, mid))(functools.partial(step, par, 1 - par, False, lat))
        if n_tiles > 1:
            pl.when(jnp.logical_and(kind, j == n_tiles))(
                functools.partial(step, None, (n_tiles - 1) % 2, False, lat))

    @pl.when(j > n_tiles)
    def _():
        o_ref[...] = x_ref[...] + gt_ref[...] * acc_ref[j - n_tiles - 1]


def _ffn(h, w_up_il, conv_w_il, w_down, layer, x, mod, gt_idx, rows):
    m, d = h.shape
    d_ff = w_down.shape[1]
    tf = _conv_tile(d_ff)
    tm = rows.row_tile()
    tq = _pick(d, 512, LANE)
    n_chunks = min(16, tm // BF16_SUBLANES)
    n_tiles = d_ff // tf
    n_out = d // tq
    u_edge = _edge_proj(h, w_up_il, layer, tm)

    def done(j):
        return jnp.clip(j - 1, 0, n_tiles - 1)

    def out_tile(j):
        return jnp.clip(j - n_tiles - 1, 0, n_out - 1)

    kern = functools.partial(_ffn_kernel, n_tiles=n_tiles, rows=rows, tm=tm, tf=tf, tq=tq, n_chunks=n_chunks)
    slot = pltpu.VMEM((tm + 2 * U_PAD, 2 * tf), F32)
    return pl.pallas_call(
        kern,
        grid=(m // tm, n_tiles + 1 + n_out),
        in_specs=[
            pl.BlockSpec((tm, d), lambda i, j: (i, 0), pipeline_mode=pl.Buffered(1)),
            pl.BlockSpec((None, d, 2 * tf), lambda i, j: (layer, 0, jnp.minimum(j, n_tiles - 1))),
            pl.BlockSpec((None, F32_SUBLANES, 2 * tf), lambda i, j: (i, 0, done(j))),
            pl.BlockSpec((None, 3, 2 * tf), lambda i, j: (layer, 0, done(j))),
            pl.BlockSpec((None, tf, d), lambda i, j: (layer, done(j), 0)),
            pl.BlockSpec((tm, tq), lambda i, j: (i, out_tile(j))),
            pl.BlockSpec((None, None, None, 1, tq),
                         lambda i, j: (layer, rows.mod_row(i, tm), gt_idx, 0, out_tile(j))),
        ],
        out_specs=pl.BlockSpec((tm, tq), lambda i, j: (i, out_tile(j))),
        out_shape=jax.ShapeDtypeStruct((m, d), F32),
        scratch_shapes=[slot, slot, pltpu.VMEM((tm, tf), BF16), pltpu.VMEM((n_out, tm, tq), F32)],
        compiler_params=_params(("arbitrary", "arbitrary")),
        name="ffn",
    )(h, w_up_il, u_edge, conv_w_il, w_down, x, mod)


def kernel(x_prompt, x_sample, cache_k, cache_v, c, c_ctx, g_mix, w_mod, b_mod, w_in, conv_w,
           g_attn_out, g_conv_out, attn_sink, w_out, g_ffn, w_up, ffn_conv_w, w_down, g_final):
    batch, seq, d = x_prompt.shape
    dec_batch, dec_seq, _ = x_sample.shape
    depth = w_in.shape[0]
    n_heads = attn_sink.shape[1]
    n_kv = cache_k.shape[3]
    attn_w, kv_w = n_heads * HEAD_DIM, n_kv * HEAD_DIM
    conv_width = d - attn_w
    d_ff = w_down.shape[1]
    qkv_w = attn_w + 2 * kv_w
    assert cache_k.shape[0] == dec_batch and cache_k.shape[4] == HEAD_DIM
    assert w_in.shape[2] == qkv_w + 3 * conv_width and w_up.shape[2] == 2 * d_ff
    assert dec_batch + 1 <= MOD_ROWS
    rows = _Rows(batch, seq, dec_batch, dec_seq)

    w_qkv_b = w_in[:, :, :qkv_w].astype(BF16)
    w_mix_b = _interleave(w_in[:, :, qkv_w:].astype(BF16), 3, _conv_tile(conv_width))
    w_up_b = _interleave(w_up.astype(BF16), 2, _conv_tile(d_ff))
    ffn_conv_il = _interleave(ffn_conv_w, 2, _conv_tile(d_ff))
    w_out_b, w_down_b = w_out.astype(BF16), w_down.astype(BF16)
    p_len = cache_k.shape[2]
    cache_k_b = cache_k.astype(BF16).reshape(dec_batch, depth, p_len, kv_w)
    cache_v_b = cache_v.astype(BF16).reshape(dec_batch, depth, p_len, kv_w)

    cvec = jnp.concatenate([c_ctx[None, :], c, jnp.zeros((MOD_ROWS - 1 - dec_batch, d), F32)], axis=0)
    mod = _adaln(cvec, w_mod, b_mod).reshape(depth, MOD_ROWS, 6, 1, d)
    tab = _rope_tables(dec_seq)

    x = jnp.concatenate([x_prompt.reshape(rows.m_ctx, d), x_sample.reshape(rows.m_lat, d)], axis=0)
    new_k, new_v = [], []
    for l in range(depth):
        h = _norm_mod(x, g_mix, mod, l, 1, 0, rows)
        qkv = _qkv_proj(h, w_qkv_b, l, attn_w + kv_w, tab, rows)
        conv = _mixer_proj(h, w_mix_b, l, conv_w, rows)
        attn = _ctx_attention(qkv, attn_sink[l], rows, n_heads, n_kv)
        attn = _lat_attention(qkv, cache_k_b, cache_v_b, l, attn_sink[l], attn, rows, n_heads, n_kv)
        y = _merge_norm(attn, conv, g_attn_out, g_conv_out, l)
        x = _resid_proj(y, w_out_b, l, x, mod, 2, rows)
        h = _norm_mod(x, g_ffn, mod, l, 4, 3, rows)
        x = _ffn(h, w_up_b, ffn_conv_il, w_down_b, l, x, mod, 5, rows)
        ctx_kv = qkv[:rows.m_ctx, attn_w:].astype(F32)
        new_k.append(ctx_kv[:, :kv_w].reshape(batch, seq, n_kv, HEAD_DIM))
        new_v.append(ctx_kv[:, kv_w:].reshape(batch, seq, n_kv, HEAD_DIM))

    y_ctx, y_lat = _final_norm(x, g_final, rows)
    y_prompt = y_ctx.reshape(batch, seq, d)
    y_sample = y_lat.reshape(dec_batch, dec_seq, d)
    return (y_prompt, y_sample, jnp.stack(new_k, axis=1), jnp.stack(new_v, axis=1))
```

```python
import functools
import math

import jax
import jax.numpy as jnp
from jax import lax
from jax.experimental import pallas as pl
from jax.experimental.pallas import tpu as pltpu

HEAD_DIM = 128
GRID_W = 64
WINDOW = 128
Q_BLOCK = 128
ROPE_THETA = 10000.0
EPS = 1e-6
NEG = -1e30

F32 = jnp.float32
BF16 = jnp.bfloat16

V7X_VMEM_BYTES = 64 * 2**20
VMEM_LIMIT_BYTES = V7X_VMEM_BYTES - 8 * 2**20
LANE = 128
F32_SUBLANES = 8
BF16_SUBLANES = 16
MOD_ROWS = F32_SUBLANES
U_PAD = F32_SUBLANES


def _pick(dim, pref, mult):
    t = (min(pref, dim) // mult) * mult
    while t > mult and dim % t:
        t -= mult
    assert t > 0 and dim % t == 0, (dim, pref, mult)
    return t


def _params(semantics):
    return pltpu.CompilerParams(dimension_semantics=semantics, vmem_limit_bytes=VMEM_LIMIT_BYTES)


def _silu(x):
    return x * (1.0 / (1.0 + jnp.exp(-x)))


class _Rows:
    def __init__(self, n_ctx_seq, ctx_len, n_lat_seq, lat_len):
        self.ctx_len, self.lat_len = ctx_len, lat_len
        self.n_ctx_seq, self.n_lat_seq = n_ctx_seq, n_lat_seq
        self.m_ctx = n_ctx_seq * ctx_len
        self.m_lat = n_lat_seq * lat_len
        self.m = self.m_ctx + self.m_lat

    def row_tile(self):
        tm = _pick(math.gcd(self.m_ctx, self.lat_len), 1024, BF16_SUBLANES)
        assert self.lat_len % tm == 0 and self.m_ctx % tm == 0
        return tm

    def mod_row(self, i, tm):
        nc = self.m_ctx // tm
        return jnp.where(i < nc, 0, 1 + (i - nc) // (self.lat_len // tm))

    def lat_block(self, i, tm):
        nc = self.m_ctx // tm
        return jnp.maximum(i - nc, 0) % (self.lat_len // tm)


def _adaln_kernel(c_ref, w_ref, b_ref, o_ref):
    s = _silu(c_ref[...]).astype(BF16)
    o_ref[...] = jnp.dot(s, w_ref[...].astype(BF16), preferred_element_type=F32) + b_ref[...]


def _adaln(cvec, w_mod, b_mod):
    depth, d, n = w_mod.shape
    tn = _pick(n, 512, LANE)
    return pl.pallas_call(
        _adaln_kernel,
        grid=(depth, n // tn),
        in_specs=[
            pl.BlockSpec((MOD_ROWS, d), lambda l, j: (0, 0)),
            pl.BlockSpec((None, d, tn), lambda l, j: (l, 0, j)),
            pl.BlockSpec((None, 1, tn), lambda l, j: (l, 0, j)),
        ],
        out_specs=pl.BlockSpec((None, MOD_ROWS, tn), lambda l, j: (l, 0, j)),
        out_shape=jax.ShapeDtypeStruct((depth, MOD_ROWS, n), F32),
        compiler_params=_params(("arbitrary", "arbitrary")),
        name="adaln",
    )(cvec, w_mod, b_mod.reshape(depth, 1, n))


def _rms(x):
    return x * lax.rsqrt(jnp.mean(x * x, axis=-1, keepdims=True) + EPS)


def _norm_mod_kernel(x_ref, g_ref, sc_ref, sh_ref, o_ref):
    y = _rms(x_ref[...]) * g_ref[...]
    o_ref[...] = (y * (1.0 + sc_ref[...]) + sh_ref[...]).astype(o_ref.dtype)


def _norm_mod(x, g, mod, layer, sc_idx, sh_idx, rows):
    m, d = x.shape
    tr = _pick(math.gcd(rows.m_ctx, rows.lat_len), 256, F32_SUBLANES)

    def mod_spec(c):
        return pl.BlockSpec((None, None, None, 1, d), lambda i: (layer, rows.mod_row(i, tr), c, 0, 0))

    return pl.pallas_call(
        _norm_mod_kernel,
        grid=(m // tr,),
        in_specs=[
            pl.BlockSpec((tr, d), lambda i: (i, 0)),
            pl.BlockSpec((None, 1, d), lambda i: (layer, 0, 0)),
            mod_spec(sc_idx),
            mod_spec(sh_idx),
        ],
        out_specs=pl.BlockSpec((tr, d), lambda i: (i, 0)),
        out_shape=jax.ShapeDtypeStruct((m, d), BF16),
        compiler_params=_params(("arbitrary",)),
        name="norm_mod",
    )(x, g.reshape(g.shape[0], 1, d), mod, mod)


def _final_norm_kernel(x_ref, g_ref, oc_ref, ol_ref, *, n_ctx_blocks):
    y = _rms(x_ref[...]) * g_ref[...]
    i = pl.program_id(0)

    @pl.when(i < n_ctx_blocks)
    def _():
        oc_ref[...] = y

    @pl.when(i >= n_ctx_blocks)
    def _():
        ol_ref[...] = y


def _final_norm(x, g, rows):
    m, d = x.shape
    tr = _pick(math.gcd(rows.m_ctx, rows.m_lat), 256, F32_SUBLANES)
    nc = rows.m_ctx // tr
    return pl.pallas_call(
        functools.partial(_final_norm_kernel, n_ctx_blocks=nc),
        grid=(m // tr,),
        in_specs=[pl.BlockSpec((tr, d), lambda i: (i, 0)), pl.BlockSpec((1, d), lambda i: (0, 0))],
        out_specs=[pl.BlockSpec((tr, d), lambda i: (jnp.minimum(i, nc - 1), 0)),
                   pl.BlockSpec((tr, d), lambda i: (jnp.maximum(i - nc, 0), 0))],
        out_shape=[jax.ShapeDtypeStruct((rows.m_ctx, d), F32), jax.ShapeDtypeStruct((rows.m_lat, d), F32)],
        compiler_params=_params(("arbitrary",)),
        name="final_norm",
    )(x, g.reshape(1, d))


def _merge_norm_kernel(a_ref, c_ref, ga_ref, gc_ref, o_ref):
    wa = a_ref.shape[-1]
    o_ref[:, :wa] = (_rms(a_ref[...].astype(F32)) * ga_ref[...]).astype(o_ref.dtype)
    o_ref[:, wa:] = (_rms(c_ref[...].astype(F32)) * gc_ref[...]).astype(o_ref.dtype)


def _merge_norm(attn, conv, g_attn, g_conv, layer):
    m, wa = attn.shape
    wc = conv.shape[1]
    tr = _pick(m, 256, BF16_SUBLANES)
    return pl.pallas_call(
        _merge_norm_kernel,
        grid=(m // tr,),
        in_specs=[
            pl.BlockSpec((tr, wa), lambda i: (i, 0)),
            pl.BlockSpec((tr, wc), lambda i: (i, 0)),
            pl.BlockSpec((None, 1, wa), lambda i: (layer, 0, 0)),
            pl.BlockSpec((None, 1, wc), lambda i: (layer, 0, 0)),
        ],
        out_specs=pl.BlockSpec((tr, wa + wc), lambda i: (i, 0)),
        out_shape=jax.ShapeDtypeStruct((m, wa + wc), BF16),
        compiler_params=_params(("arbitrary",)),
        name="merge_norm",
    )(attn, conv, g_attn.reshape(-1, 1, wa), g_conv.reshape(-1, 1, wc))


def _rope_tables(t_len):
    pos = jnp.arange(t_len)
    row = (pos // GRID_W).astype(F32)
    col = (pos % GRID_W).astype(F32)
    n_freq = HEAD_DIM // 4
    inv = ROPE_THETA ** (-jnp.arange(n_freq, dtype=F32) / n_freq)
    ar = row[:, None] * inv
    ac = col[:, None] * inv
    ang = jnp.concatenate([ar, ar, ac, ac], axis=-1)
    cos, sin = jnp.cos(ang), jnp.sin(ang)
    lower = (jnp.arange(HEAD_DIM) % (2 * n_freq)) < n_freq
    return jnp.concatenate([cos, jnp.where(lower, -sin, 0.0), jnp.where(lower, 0.0, sin)], axis=-1)


def _qkv_kernel(h_ref, w_ref, tab_ref, o_ref, *, n_ctx_blocks, rope_heads):
    i, j = pl.program_id(0), pl.program_id(1)
    acc = jnp.dot(h_ref[...], w_ref[...], preferred_element_type=F32)
    heads = o_ref.shape[-1] // HEAD_DIM

    def emit(n_rope):
        for hc in range(heads):
            cols = slice(hc * HEAD_DIM, (hc + 1) * HEAD_DIM)
            a = acc[:, cols]
            if hc < n_rope:
                cos = tab_ref[:, :HEAD_DIM]
                sin_up = tab_ref[:, HEAD_DIM:2 * HEAD_DIM]
                sin_dn = tab_ref[:, 2 * HEAD_DIM:]
                a = (a * cos + pltpu.roll(a, HEAD_DIM - HEAD_DIM // 4, 1) * sin_up
                     + pltpu.roll(a, HEAD_DIM // 4, 1) * sin_dn)
            o_ref[:, cols] = a.astype(o_ref.dtype)

    is_lat = i >= n_ctx_blocks
    for n_rope in sorted(set(rope_heads) - {0}):
        tiles = [jj for jj, n in enumerate(rope_heads) if n == n_rope]
        hit = functools.reduce(jnp.logical_or, [j == jj for jj in tiles])
        pl.when(jnp.logical_and(is_lat, hit))(functools.partial(emit, n_rope))
    plain = [jj for jj, n in enumerate(rope_heads) if n == 0]
    no_rope = jnp.logical_not(is_lat)
    if plain:
        no_rope = jnp.logical_or(no_rope, functools.reduce(jnp.logical_or, [j == jj for jj in plain]))
    pl.when(no_rope)(functools.partial(emit, 0))


def _qkv_proj(h, w_qkv, layer, n_cols, n_rope_cols, tab, rows):
    m, d = h.shape
    tm = rows.row_tile()
    tn = _pick(n_cols, 1024, HEAD_DIM)
    n_tiles = n_cols // tn
    rope_heads = tuple(min(max(n_rope_cols - jj * tn, 0), tn) // HEAD_DIM for jj in range(n_tiles))
    kern = functools.partial(_qkv_kernel, n_ctx_blocks=rows.m_ctx // tm, rope_heads=rope_heads)
    return pl.pallas_call(
        kern,
        grid=(m // tm, n_tiles),
        in_specs=[
            pl.BlockSpec((tm, d), lambda i, j: (i, 0)),
            pl.BlockSpec((None, d, tn), lambda i, j: (layer, 0, j)),
            pl.BlockSpec((tm, 3 * HEAD_DIM), lambda i, j: (rows.lat_block(i, tm), 0)),
        ],
        out_specs=pl.BlockSpec((tm, tn), lambda i, j: (i, j)),
        out_shape=jax.ShapeDtypeStruct((m, n_cols), BF16),
        compiler_params=_params(("arbitrary", "arbitrary")),
        name="qkv_proj",
    )(h, w_qkv, tab)


def _edge_kernel(h_ref, w_ref, o_ref):
    o_ref[...] = jnp.dot(h_ref[...], w_ref[...], preferred_element_type=F32)


def _edge_proj(h, w, layer, col0, n_cols, tm):
    m, d = h.shape
    nblk = m // tm
    idx = []
    for i in range(nblk):
        idx += [max(i * tm - 1, 0), min((i + 1) * tm, m - 1)]
    n_rows = -(-len(idx) // BF16_SUBLANES) * BF16_SUBLANES
    idx += [0] * (n_rows - len(idx))
    h_edge = jnp.take(h, jnp.asarray(idx, jnp.int32), axis=0)
    tn = _pick(math.gcd(n_cols, col0) if col0 else n_cols, 512, LANE)
    u = pl.pallas_call(
        _edge_kernel,
        grid=(n_cols // tn,),
        in_specs=[
            pl.BlockSpec((n_rows, d), lambda j: (0, 0)),
            pl.BlockSpec((None, d, tn), lambda j: (layer, 0, col0 // tn + j)),
        ],
        out_specs=pl.BlockSpec((n_rows, tn), lambda j: (0, j)),
        out_shape=jax.ShapeDtypeStruct((n_rows, n_cols), F32),
        compiler_params=_params(("arbitrary",)),
        name="edge_proj",
    )(h_edge, w)
    return jnp.pad(u[:2 * nblk].reshape(nblk, 2, n_cols), ((0, 0), (0, F32_SUBLANES - 2), (0, 0)))


def _place_edges(u_ref, ue_ref, i, lat, rows, tm):
    before, after = ue_ref[0:1, :], ue_ref[1:2, :]
    if lat:
        seq_row = ((i - rows.m_ctx // tm) * tm) % rows.lat_len
        before = jnp.where(seq_row != 0, before, 0.0)
        after = jnp.where(seq_row + tm != rows.lat_len, after, 0.0)
    u_ref[U_PAD - 1:U_PAD, :] = before
    u_ref[U_PAD + tm:U_PAD + tm + 1, :] = after


def _conv_rows(x, cw_ref, lat, row0, rows):
    n_ext = x.shape[0]
    rc = n_ext - 2 * U_PAD
    prev = pltpu.roll(x, 1, 0)[U_PAD:U_PAD + rc]
    nxt = pltpu.roll(x, n_ext - 1, 0)[U_PAD:U_PAD + rc]
    if not lat:
        pos = (row0 + lax.broadcasted_iota(jnp.int32, (rc, 1), 0)) % rows.ctx_len
        prev = jnp.where(pos != 0, prev, 0.0)
        nxt = jnp.where(pos != rows.ctx_len - 1, nxt, 0.0)
    return cw_ref[0:1, :] * prev + cw_ref[1:2, :] * x[U_PAD:U_PAD + rc] + cw_ref[2:3, :] * nxt


def _gated_conv_kernel(h_ref, *refs, mode, n_tiles, rows, tm, n_chunks, k_tail):
    n_w = 3 if mode == "mixer" else 2
    w_refs = refs[:n_w]
    ue_refs = refs[n_w:n_w + 2]
    cw_refs = refs[n_w + 2:-1 - n_w]
    o_ref = refs[-1 - n_w]
    u_refs = refs[-n_w:]
    conv_u = u_refs[-2:]
    i, j = pl.program_id(0), pl.program_id(1)
    rc = tm // n_chunks
    cur = j % 2
    prv = 1 - cur
    d = h_ref.shape[-1]

    def matmuls(zero):
        h = h_ref[...]
        for w_ref, u_ref in zip(w_refs, u_refs):
            if zero is None:
                u = jnp.dot(h, w_ref[...], preferred_element_type=F32)
            else:
                k0 = d - k_tail
                u = (jnp.dot(h[:, :k0], w_ref[:k0, :], preferred_element_type=F32)
                     + jnp.dot(h[:, k0:], w_ref[k0:, :] + zero, preferred_element_type=F32))
            pad = U_PAD if u_ref.shape[1] > tm else 0
            u_ref[cur, pad:pad + tm, :] = u

    def finish(lat):
        for u_ref, ue_ref in zip(conv_u, ue_refs):
            _place_edges(u_ref.at[prv], ue_ref, i, lat, rows, tm)
        bits = None
        for c in range(n_chunks):
            lo = c * rc
            ext = pl.ds(lo, rc + 2 * U_PAD)
            if mode == "mixer":
                y = _conv_rows(conv_u[0][prv, ext, :] * conv_u[1][prv, ext, :], cw_refs[0], lat, i * tm + lo, rows)
                out = u_refs[0][prv, pl.ds(lo, rc), :] * y
            else:
                g = _conv_rows(conv_u[0][prv, ext, :], cw_refs[0], lat, i * tm + lo, rows)
                v = _conv_rows(conv_u[1][prv, ext, :], cw_refs[1], lat, i * tm + lo, rows)
                out = _silu(g) * v
            o_ref[lo:lo + rc, :] = out.astype(o_ref.dtype)
            b = pltpu.bitcast(out[0:F32_SUBLANES, :], jnp.uint32)
            bits = b if bits is None else bits | b
        z = pltpu.bitcast((bits >> 16) >> 16, F32)
        return z[0:1, :].astype(BF16)

    def step(lat, first, last):
        zero = None if first else finish(lat)
        if not last:
            matmuls(zero)

    is_lat = i >= rows.m_ctx // tm
    pl.when(j == 0)(functools.partial(step, True, True, False))
    for lat in (False, True):
        kind = is_lat if lat else jnp.logical_not(is_lat)
        pl.when(jnp.logical_and(kind, jnp.logical_and(j > 0, j < n_tiles)))(
            functools.partial(step, lat, False, False))
        pl.when(jnp.logical_and(kind, j == n_tiles))(functools.partial(step, lat, False, True))


def _gated_conv_proj(h, w, layer, col0, conv_w, mode, rows):
    m, d = h.shape
    n_w = 3 if mode == "mixer" else 2
    n_cw = 1 if mode == "mixer" else 2
    width = conv_w.shape[-1] // n_cw
    tf = _pick(math.gcd(width, col0) if col0 else width, 256, LANE)
    tm = rows.row_tile()
    n_chunks = min(16, tm // BF16_SUBLANES)
    n_tiles = width // tf
    conv_col0 = col0 + (n_w - 2) * width
    u_edge = _edge_proj(h, w, layer, conv_col0, 2 * width, tm)

    def tile(j):
        return jnp.minimum(j, n_tiles - 1)

    def done(j):
        return jnp.maximum(j - 1, 0)

    def groups(n, shape, index):
        return [pl.BlockSpec(shape, functools.partial(index, g * n_tiles)) for g in range(n)]

    kern = functools.partial(_gated_conv_kernel, mode=mode, n_tiles=n_tiles, rows=rows, tm=tm,
                             n_chunks=n_chunks, k_tail=min(256, d // 2))
    plain = pltpu.VMEM((2, tm, tf), F32)
    padded = pltpu.VMEM((2, tm + 2 * U_PAD, tf), F32)
    return pl.pallas_call(
        kern,
        grid=(m // tm, n_tiles + 1),
        in_specs=[pl.BlockSpec((tm, d), lambda i, j: (i, 0))]
        + groups(n_w, (None, d, tf), lambda off, i, j: (layer, 0, col0 // tf + off + tile(j)))
        + groups(2, (None, F32_SUBLANES, tf), lambda off, i, j: (i, 0, off + done(j)))
        + groups(n_cw, (None, 3, tf), lambda off, i, j: (layer, 0, off + done(j))),
        out_specs=pl.BlockSpec((tm, tf), lambda i, j: (i, done(j))),
        out_shape=jax.ShapeDtypeStruct((m, width), BF16),
        scratch_shapes=[plain] * (n_w - 2) + [padded, padded],
        compiler_params=_params(("arbitrary", "arbitrary")),
        name="gated_conv_" + mode,
    )(h, *([w] * n_w), u_edge, u_edge, *([conv_w] * n_cw))


def _resid_kernel(a_ref, w_ref, x_ref, gt_ref, o_ref):
    p = jnp.dot(a_ref[...], w_ref[...], preferred_element_type=F32)
    o_ref[...] = x_ref[...] + gt_ref[...] * p


def _resid_proj(a, w, layer, x, mod, gt_idx, rows, tn_pref):
    m, k = a.shape
    n = w.shape[-1]
    tm = rows.row_tile()
    tn = _pick(n, tn_pref, LANE)
    return pl.pallas_call(
        _resid_kernel,
        grid=(m // tm, n // tn),
        in_specs=[
            pl.BlockSpec((tm, k), lambda i, j: (i, 0), pipeline_mode=pl.Buffered(1)),
            pl.BlockSpec((None, k, tn), lambda i, j: (layer, 0, j)),
            pl.BlockSpec((tm, tn), lambda i, j: (i, j)),
            pl.BlockSpec((None, None, None, 1, tn),
                         lambda i, j: (layer, rows.mod_row(i, tm), gt_idx, 0, j)),
        ],
        out_specs=pl.BlockSpec((tm, tn), lambda i, j: (i, j)),
        out_shape=jax.ShapeDtypeStruct((m, n), F32),
        compiler_params=_params(("arbitrary", "arbitrary")),
        name="resid_proj",
    )(a, w, x, mod)


def _stack_heads(q_ref, kv_head, group):
    base = kv_head * group
    return jnp.concatenate(
        [q_ref[:, (base + g) * HEAD_DIM:(base + g + 1) * HEAD_DIM] for g in range(group)], axis=0)


def _sink_column(sink_ref, kv_head, group, rows_per_head):
    rg = lax.broadcasted_iota(jnp.int32, (group * rows_per_head, 1), 0) // rows_per_head
    sk = jnp.zeros((group * rows_per_head, 1), F32)
    for g in range(group):
        sk = jnp.where(rg == g, sink_ref[kv_head * group + g], sk)
    return sk


def _qk(q, k):
    return lax.dot_general(q, k, (((1,), (1,)), ((), ())), preferred_element_type=F32)


def _ctx_attn_kernel(sink_ref, q_ref, k_ref, v_ref, o_ref, *, group, n_kv, scale):
    s_len = q_ref.shape[0]
    for h in range(n_kv):
        q = _stack_heads(q_ref, h, group)
        k = k_ref[:, h * HEAD_DIM:(h + 1) * HEAD_DIM]
        v = v_ref[:, h * HEAD_DIM:(h + 1) * HEAD_DIM]
        s = _qk(q, k) * scale
        sk = _sink_column(sink_ref, h, group, s_len)
        mx = jnp.maximum(jnp.max(s, axis=-1, keepdims=True), sk)
        e = jnp.exp(s - mx)
        inv = 1.0 / (jnp.sum(e, axis=-1, keepdims=True) + jnp.exp(sk - mx))
        o = jnp.dot((e * inv).astype(BF16), v, preferred_element_type=F32)
        for g in range(group):
            c0 = (h * group + g) * HEAD_DIM
            o_ref[:, c0:c0 + HEAD_DIM] = o[g * s_len:(g + 1) * s_len].astype(o_ref.dtype)


def _ctx_attention(qkv, sink, rows, n_heads, n_kv):
    m = qkv.shape[0]
    attn_w, kv_w = n_heads * HEAD_DIM, n_kv * HEAD_DIM
    s_len = rows.ctx_len
    kern = functools.partial(_ctx_attn_kernel, group=n_heads // n_kv, n_kv=n_kv, scale=HEAD_DIM ** -0.5)
    return pl.pallas_call(
        kern,
        grid=(rows.n_ctx_seq,),
        in_specs=[
            pl.BlockSpec(memory_space=pltpu.SMEM),
            pl.BlockSpec((s_len, attn_w), lambda b: (b, 0)),
            pl.BlockSpec((s_len, kv_w), lambda b: (b, attn_w // kv_w)),
            pl.BlockSpec((s_len, kv_w), lambda b: (b, attn_w // kv_w + 1)),
        ],
        out_specs=pl.BlockSpec((s_len, attn_w), lambda b: (b, 0)),
        out_shape=jax.ShapeDtypeStruct((m, attn_w), BF16),
        compiler_params=_params(("arbitrary",)),
        name="ctx_attention",
    )(sink, qkv, qkv, qkv)


def _lat_attn_kernel(sink_ref, q_ref, k_ref, v_ref, kc_ref, vc_ref, prev_ref, o_ref, *,
                     group, n_kv, t_len, scale):
    del prev_ref
    blk = pl.program_id(1)
    band = 3 * Q_BLOCK
    start = pl.multiple_of(jnp.clip((blk - 1) * Q_BLOCK, 0, t_len - band), Q_BLOCK)
    kb = k_ref[pl.ds(start, band), :]
    vb = v_ref[pl.ds(start, band), :]
    n_rows = group * Q_BLOCK
    qpos = blk * Q_BLOCK + lax.broadcasted_iota(jnp.int32, (n_rows, band), 0) % Q_BLOCK
    kpos = start + lax.broadcasted_iota(jnp.int32, (n_rows, band), 1)
    valid = jnp.abs(kpos - qpos) <= WINDOW
    for h in range(n_kv):
        cols = slice(h * HEAD_DIM, (h + 1) * HEAD_DIM)
        q = _stack_heads(q_ref, h, group)
        s_c = _qk(q, kc_ref[:, cols]) * scale
        s_b = jnp.where(valid, _qk(q, kb[:, cols]) * scale, NEG)
        sk = _sink_column(sink_ref, h, group, Q_BLOCK)
        mx = jnp.maximum(jnp.maximum(jnp.max(s_c, axis=-1, keepdims=True),
                                     jnp.max(s_b, axis=-1, keepdims=True)), sk)
        e_c = jnp.exp(s_c - mx)
        e_b = jnp.exp(s_b - mx)
        inv = 1.0 / (jnp.sum(e_c, axis=-1, keepdims=True) + jnp.sum(e_b, axis=-1, keepdims=True)
                     + jnp.exp(sk - mx))
        o = (jnp.dot((e_c * inv).astype(BF16), vc_ref[:, cols], preferred_element_type=F32)
             + jnp.dot((e_b * inv).astype(BF16), vb[:, cols], preferred_element_type=F32))
        for g in range(group):
            c0 = (h * group + g) * HEAD_DIM
            o_ref[:, c0:c0 + HEAD_DIM] = o[g * Q_BLOCK:(g + 1) * Q_BLOCK].astype(o_ref.dtype)


def _lat_attention(qkv, cache_k, cache_v, layer, sink, attn_prev, rows, n_heads, n_kv):
    attn_w, kv_w = n_heads * HEAD_DIM, n_kv * HEAD_DIM
    t_len = rows.lat_len
    p_len = cache_k.shape[2]
    assert rows.m_ctx % t_len == 0 and t_len >= 3 * Q_BLOCK and t_len % Q_BLOCK == 0
    nblk = t_len // Q_BLOCK
    q0 = rows.m_ctx // Q_BLOCK
    s0 = rows.m_ctx // t_len
    kern = functools.partial(_lat_attn_kernel, group=n_heads // n_kv, n_kv=n_kv, t_len=t_len,
                             scale=HEAD_DIM ** -0.5)
    return pl.pallas_call(
        kern,
        grid=(rows.n_lat_seq, nblk),
        in_specs=[
            pl.BlockSpec(memory_space=pltpu.SMEM),
            pl.BlockSpec((Q_BLOCK, attn_w), lambda b, q: (q0 + b * nblk + q, 0)),
            pl.BlockSpec((t_len, kv_w), lambda b, q: (s0 + b, attn_w // kv_w)),
            pl.BlockSpec((t_len, kv_w), lambda b, q: (s0 + b, attn_w // kv_w + 1)),
            pl.BlockSpec((None, None, p_len, kv_w), lambda b, q: (b, layer, 0, 0)),
            pl.BlockSpec((None, None, p_len, kv_w), lambda b, q: (b, layer, 0, 0)),
            pl.BlockSpec(memory_space=pl.ANY),
        ],
        out_specs=pl.BlockSpec((Q_BLOCK, attn_w), lambda b, q: (q0 + b * nblk + q, 0)),
        out_shape=jax.ShapeDtypeStruct(attn_prev.shape, attn_prev.dtype),
        input_output_aliases={6: 0},
        compiler_params=_params(("arbitrary", "arbitrary")),
        name="lat_attention",
    )(sink, qkv, qkv, qkv, cache_k, cache_v, attn_prev)


def kernel(x_prompt, x_sample, cache_k, cache_v, c, c_ctx, g_mix, w_mod, b_mod, w_in, conv_w,
           g_attn_out, g_conv_out, attn_sink, w_out, g_ffn, w_up, ffn_conv_w, w_down, g_final):
    batch, seq, d = x_prompt.shape
    dec_batch, dec_seq, _ = x_sample.shape
    depth = w_in.shape[0]
    n_heads = attn_sink.shape[1]
    n_kv = cache_k.shape[3]
    attn_w, kv_w = n_heads * HEAD_DIM, n_kv * HEAD_DIM
    conv_width = d - attn_w
    d_ff = w_down.shape[1]
    qkv_w = attn_w + 2 * kv_w
    assert cache_k.shape[0] == dec_batch and cache_k.shape[4] == HEAD_DIM
    assert w_in.shape[2] == qkv_w + 3 * conv_width and w_up.shape[2] == 2 * d_ff
    assert dec_batch + 1 <= MOD_ROWS
    rows = _Rows(batch, seq, dec_batch, dec_seq)

    w_in_b, w_out_b, w_up_b, w_down_b = (w.astype(BF16) for w in (w_in, w_out, w_up, w_down))
    p_len = cache_k.shape[2]
    cache_k_b = cache_k.astype(BF16).reshape(dec_batch, depth, p_len, kv_w)
    cache_v_b = cache_v.astype(BF16).reshape(dec_batch, depth, p_len, kv_w)

    cvec = jnp.concatenate([c_ctx[None, :], c, jnp.zeros((MOD_ROWS - 1 - dec_batch, d), F32)], axis=0)
    mod = _adaln(cvec, w_mod, b_mod).reshape(depth, MOD_ROWS, 6, 1, d)
    tab = _rope_tables(dec_seq)

    x = jnp.concatenate([x_prompt.reshape(rows.m_ctx, d), x_sample.reshape(rows.m_lat, d)], axis=0)
    new_k, new_v = [], []
    for l in range(depth):
        h = _norm_mod(x, g_mix, mod, l, 1, 0, rows)
        qkv = _qkv_proj(h, w_in_b, l, qkv_w, attn_w + kv_w, tab, rows)
        conv = _gated_conv_proj(h, w_in_b, l, qkv_w, conv_w, "mixer", rows)
        attn = _ctx_attention(qkv, attn_sink[l], rows, n_heads, n_kv)
        attn = _lat_attention(qkv, cache_k_b, cache_v_b, l, attn_sink[l], attn, rows, n_heads, n_kv)
        y = _merge_norm(attn, conv, g_attn_out, g_conv_out, l)
        x = _resid_proj(y, w_out_b, l, x, mod, 2, rows, 1024)
        h = _norm_mod(x, g_ffn, mod, l, 4, 3, rows)
        act = _gated_conv_proj(h, w_up_b, l, 0, ffn_conv_w, "ffn", rows)
        x = _resid_proj(act, w_down_b, l, x, mod, 5, rows, 256)
        ctx_kv = qkv[:rows.m_ctx, attn_w:].astype(F32)
        new_k.append(ctx_kv[:, :kv_w].reshape(batch, seq, n_kv, HEAD_DIM))
        new_v.append(ctx_kv[:, kv_w:].reshape(batch, seq, n_kv, HEAD_DIM))

    y_ctx, y_lat = _final_norm(x, g_final, rows)
    y_prompt = y_ctx.reshape(batch, seq, d)
    y_sample = y_lat.reshape(dec_batch, dec_seq, d)
    return (y_prompt, y_sample, jnp.stack(new_k, axis=1), jnp.stack(new_v, axis=1))
```

```python
import functools
import math

import jax
import jax.numpy as jnp
from jax import lax
from jax.experimental import pallas as pl
from jax.experimental.pallas import tpu as pltpu

HEAD_DIM = 128
GRID_W = 64
WINDOW = 128
Q_BLOCK = 128
ROPE_THETA = 10000.0
EPS = 1e-6
NEG = -1e30

F32 = jnp.float32
BF16 = jnp.bfloat16

V7X_VMEM_BYTES = 64 * 2**20
VMEM_LIMIT_BYTES = V7X_VMEM_BYTES - 8 * 2**20
LANE = 128
F32_SUBLANES = 8
BF16_SUBLANES = 16
MOD_ROWS = F32_SUBLANES
U_PAD = F32_SUBLANES


def _pick(dim, pref, mult):
    t = (min(pref, dim) // mult) * mult
    while t > mult and dim % t:
        t -= mult
    assert t > 0 and dim % t == 0, (dim, pref, mult)
    return t


def _params(semantics):
    return pltpu.CompilerParams(dimension_semantics=semantics, vmem_limit_bytes=VMEM_LIMIT_BYTES)


def _silu(x):
    return x * (1.0 / (1.0 + jnp.exp(-x)))


class _Rows:
    def __init__(self, n_ctx_seq, ctx_len, n_lat_seq, lat_len):
        self.ctx_len, self.lat_len = ctx_len, lat_len
        self.n_ctx_seq, self.n_lat_seq = n_ctx_seq, n_lat_seq
        self.m_ctx = n_ctx_seq * ctx_len
        self.m_lat = n_lat_seq * lat_len
        self.m = self.m_ctx + self.m_lat

    def row_tile(self):
        tm = _pick(math.gcd(self.m_ctx, self.lat_len), 1024, BF16_SUBLANES)
        assert self.lat_len % tm == 0 and self.m_ctx % tm == 0
        return tm

    def mod_row(self, i, tm):
        nc = self.m_ctx // tm
        return jnp.where(i < nc, 0, 1 + (i - nc) // (self.lat_len // tm))

    def lat_block(self, i, tm):
        nc = self.m_ctx // tm
        return jnp.maximum(i - nc, 0) % (self.lat_len // tm)


def _adaln_kernel(c_ref, w_ref, b_ref, o_ref):
    s = _silu(c_ref[...]).astype(BF16)
    o_ref[...] = jnp.dot(s, w_ref[...].astype(BF16), preferred_element_type=F32) + b_ref[...]


def _adaln(cvec, w_mod, b_mod):
    depth, d, n = w_mod.shape
    tn = _pick(n, 512, LANE)
    return pl.pallas_call(
        _adaln_kernel,
        grid=(depth, n // tn),
        in_specs=[
            pl.BlockSpec((MOD_ROWS, d), lambda l, j: (0, 0)),
            pl.BlockSpec((None, d, tn), lambda l, j: (l, 0, j)),
            pl.BlockSpec((None, 1, tn), lambda l, j: (l, 0, j)),
        ],
        out_specs=pl.BlockSpec((None, MOD_ROWS, tn), lambda l, j: (l, 0, j)),
        out_shape=jax.ShapeDtypeStruct((depth, MOD_ROWS, n), F32),
        compiler_params=_params(("arbitrary", "arbitrary")),
        name="adaln",
    )(cvec, w_mod, b_mod.reshape(depth, 1, n))


def _rms(x):
    return x * lax.rsqrt(jnp.mean(x * x, axis=-1, keepdims=True) + EPS)


def _norm_mod_kernel(x_ref, g_ref, sc_ref, sh_ref, o_ref):
    y = _rms(x_ref[...]) * g_ref[...]
    o_ref[...] = (y * (1.0 + sc_ref[...]) + sh_ref[...]).astype(o_ref.dtype)


def _norm_mod(x, g, mod, layer, sc_idx, sh_idx, rows):
    m, d = x.shape
    tr = _pick(math.gcd(rows.m_ctx, rows.lat_len), 256, F32_SUBLANES)

    def mod_spec(c):
        return pl.BlockSpec((None, None, None, 1, d), lambda i: (layer, rows.mod_row(i, tr), c, 0, 0))

    return pl.pallas_call(
        _norm_mod_kernel,
        grid=(m // tr,),
        in_specs=[
            pl.BlockSpec((tr, d), lambda i: (i, 0)),
            pl.BlockSpec((None, 1, d), lambda i: (layer, 0, 0)),
            mod_spec(sc_idx),
            mod_spec(sh_idx),
        ],
        out_specs=pl.BlockSpec((tr, d), lambda i: (i, 0)),
        out_shape=jax.ShapeDtypeStruct((m, d), BF16),
        compiler_params=_params(("arbitrary",)),
        name="norm_mod",
    )(x, g.reshape(g.shape[0], 1, d), mod, mod)


def _final_norm_kernel(x_ref, g_ref, oc_ref, ol_ref, *, n_ctx_blocks):
    y = _rms(x_ref[...]) * g_ref[...]
    i = pl.program_id(0)

    @pl.when(i < n_ctx_blocks)
    def _():
        oc_ref[...] = y

    @pl.when(i >= n_ctx_blocks)
    def _():
        ol_ref[...] = y


def _final_norm(x, g, rows):
    m, d = x.shape
    tr = _pick(math.gcd(rows.m_ctx, rows.m_lat), 256, F32_SUBLANES)
    nc = rows.m_ctx // tr
    return pl.pallas_call(
        functools.partial(_final_norm_kernel, n_ctx_blocks=nc),
        grid=(m // tr,),
        in_specs=[pl.BlockSpec((tr, d), lambda i: (i, 0)), pl.BlockSpec((1, d), lambda i: (0, 0))],
        out_specs=[pl.BlockSpec((tr, d), lambda i: (jnp.minimum(i, nc - 1), 0)),
                   pl.BlockSpec((tr, d), lambda i: (jnp.maximum(i - nc, 0), 0))],
        out_shape=[jax.ShapeDtypeStruct((rows.m_ctx, d), F32), jax.ShapeDtypeStruct((rows.m_lat, d), F32)],
        compiler_params=_params(("arbitrary",)),
        name="final_norm",
    )(x, g.reshape(1, d))


def _merge_norm_kernel(a_ref, c_ref, ga_ref, gc_ref, o_ref):
    wa = a_ref.shape[-1]
    o_ref[:, :wa] = (_rms(a_ref[...].astype(F32)) * ga_ref[...]).astype(o_ref.dtype)
    o_ref[:, wa:] = (_rms(c_ref[...].astype(F32)) * gc_ref[...]).astype(o_ref.dtype)


def _merge_norm(attn, conv, g_attn, g_conv, layer):
    m, wa = attn.shape
    wc = conv.shape[1]
    tr = _pick(m, 256, BF16_SUBLANES)
    return pl.pallas_call(
        _merge_norm_kernel,
        grid=(m // tr,),
        in_specs=[
            pl.BlockSpec((tr, wa), lambda i: (i, 0)),
            pl.BlockSpec((tr, wc), lambda i: (i, 0)),
            pl.BlockSpec((None, 1, wa), lambda i: (layer, 0, 0)),
            pl.BlockSpec((None, 1, wc), lambda i: (layer, 0, 0)),
        ],
        out_specs=pl.BlockSpec((tr, wa + wc), lambda i: (i, 0)),
        out_shape=jax.ShapeDtypeStruct((m, wa + wc), BF16),
        compiler_params=_params(("arbitrary",)),
        name="merge_norm",
    )(attn, conv, g_attn.reshape(-1, 1, wa), g_conv.reshape(-1, 1, wc))


def _rope_tables(t_len):
    pos = jnp.arange(t_len)
    row = (pos // GRID_W).astype(F32)
    col = (pos % GRID_W).astype(F32)
    n_freq = HEAD_DIM // 4
    inv = ROPE_THETA ** (-jnp.arange(n_freq, dtype=F32) / n_freq)
    ar = row[:, None] * inv
    ac = col[:, None] * inv
    ang = jnp.concatenate([ar, ar, ac, ac], axis=-1)
    cos, sin = jnp.cos(ang), jnp.sin(ang)
    lower = (jnp.arange(HEAD_DIM) % (2 * n_freq)) < n_freq
    return jnp.concatenate([cos, jnp.where(lower, -sin, 0.0), jnp.where(lower, 0.0, sin)], axis=-1)


def _qkv_kernel(h_ref, w_ref, tab_ref, o_ref, *, n_ctx_blocks, rope_heads):
    i, j = pl.program_id(0), pl.program_id(1)
    acc = jnp.dot(h_ref[...], w_ref[...], preferred_element_type=F32)
    heads = o_ref.shape[-1] // HEAD_DIM

    def emit(n_rope):
        for hc in range(heads):
            cols = slice(hc * HEAD_DIM, (hc + 1) * HEAD_DIM)
            a = acc[:, cols]
            if hc < n_rope:
                cos = tab_ref[:, :HEAD_DIM]
                sin_up = tab_ref[:, HEAD_DIM:2 * HEAD_DIM]
                sin_dn = tab_ref[:, 2 * HEAD_DIM:]
                a = (a * cos + pltpu.roll(a, HEAD_DIM - HEAD_DIM // 4, 1) * sin_up
                     + pltpu.roll(a, HEAD_DIM // 4, 1) * sin_dn)
            o_ref[:, cols] = a.astype(o_ref.dtype)

    is_lat = i >= n_ctx_blocks
    for n_rope in sorted(set(rope_heads) - {0}):
        tiles = [jj for jj, n in enumerate(rope_heads) if n == n_rope]
        hit = functools.reduce(jnp.logical_or, [j == jj for jj in tiles])
        pl.when(jnp.logical_and(is_lat, hit))(functools.partial(emit, n_rope))
    plain = [jj for jj, n in enumerate(rope_heads) if n == 0]
    no_rope = jnp.logical_not(is_lat)
    if plain:
        no_rope = jnp.logical_or(no_rope, functools.reduce(jnp.logical_or, [j == jj for jj in plain]))
    pl.when(no_rope)(functools.partial(emit, 0))


def _qkv_proj(h, w_qkv, layer, n_cols, n_rope_cols, tab, rows):
    m, d = h.shape
    tm = rows.row_tile()
    tn = _pick(n_cols, 1024, HEAD_DIM)
    n_tiles = n_cols // tn
    rope_heads = tuple(min(max(n_rope_cols - jj * tn, 0), tn) // HEAD_DIM for jj in range(n_tiles))
    kern = functools.partial(_qkv_kernel, n_ctx_blocks=rows.m_ctx // tm, rope_heads=rope_heads)
    return pl.pallas_call(
        kern,
        grid=(m // tm, n_tiles),
        in_specs=[
            pl.BlockSpec((tm, d), lambda i, j: (i, 0)),
            pl.BlockSpec((None, d, tn), lambda i, j: (layer, 0, j)),
            pl.BlockSpec((tm, 3 * HEAD_DIM), lambda i, j: (rows.lat_block(i, tm), 0)),
        ],
        out_specs=pl.BlockSpec((tm, tn), lambda i, j: (i, j)),
        out_shape=jax.ShapeDtypeStruct((m, n_cols), BF16),
        compiler_params=_params(("arbitrary", "arbitrary")),
        name="qkv_proj",
    )(h, w_qkv, tab)


def _edge_kernel(h_ref, w_ref, o_ref):
    o_ref[...] = jnp.dot(h_ref[...], w_ref[...], preferred_element_type=F32)


def _edge_proj(h, w, layer, col0, n_cols, tm):
    m, d = h.shape
    nblk = m // tm
    idx = []
    for i in range(nblk):
        idx += [max(i * tm - 1, 0), min((i + 1) * tm, m - 1)]
    n_rows = -(-len(idx) // BF16_SUBLANES) * BF16_SUBLANES
    idx += [0] * (n_rows - len(idx))
    h_edge = jnp.take(h, jnp.asarray(idx, jnp.int32), axis=0)
    tn = _pick(math.gcd(n_cols, col0) if col0 else n_cols, 512, LANE)
    u = pl.pallas_call(
        _edge_kernel,
        grid=(n_cols // tn,),
        in_specs=[
            pl.BlockSpec((n_rows, d), lambda j: (0, 0)),
            pl.BlockSpec((None, d, tn), lambda j: (layer, 0, col0 // tn + j)),
        ],
        out_specs=pl.BlockSpec((n_rows, tn), lambda j: (0, j)),
        out_shape=jax.ShapeDtypeStruct((n_rows, n_cols), F32),
        compiler_params=_params(("arbitrary",)),
        name="edge_proj",
    )(h_edge, w)
    return jnp.pad(u[:2 * nblk].reshape(nblk, 2, n_cols), ((0, 0), (0, F32_SUBLANES - 2), (0, 0)))


def _place_edges(u_ref, ue_ref, i, lat, rows, tm):
    before, after = ue_ref[0:1, :], ue_ref[1:2, :]
    if lat:
        seq_row = ((i - rows.m_ctx // tm) * tm) % rows.lat_len
        before = jnp.where(seq_row != 0, before, 0.0)
        after = jnp.where(seq_row + tm != rows.lat_len, after, 0.0)
    u_ref[U_PAD - 1:U_PAD, :] = before
    u_ref[U_PAD + tm:U_PAD + tm + 1, :] = after


def _conv_rows(x, cw_ref, lat, row0, rows):
    n_ext = x.shape[0]
    rc = n_ext - 2 * U_PAD
    prev = pltpu.roll(x, 1, 0)[U_PAD:U_PAD + rc]
    nxt = pltpu.roll(x, n_ext - 1, 0)[U_PAD:U_PAD + rc]
    if not lat:
        pos = (row0 + lax.broadcasted_iota(jnp.int32, (rc, 1), 0)) % rows.ctx_len
        prev = jnp.where(pos != 0, prev, 0.0)
        nxt = jnp.where(pos != rows.ctx_len - 1, nxt, 0.0)
    return cw_ref[0:1, :] * prev + cw_ref[1:2, :] * x[U_PAD:U_PAD + rc] + cw_ref[2:3, :] * nxt


def _gated_conv_kernel(h_ref, *refs, mode, n_tiles, rows, tm, n_chunks, k_tail):
    n_w = 3 if mode == "mixer" else 2
    w_refs = refs[:n_w]
    ue_refs = refs[n_w:n_w + 2]
    cw_refs = refs[n_w + 2:-1 - n_w]
    o_ref = refs[-1 - n_w]
    u_refs = refs[-n_w:]
    conv_u = u_refs[-2:]
    i, j = pl.program_id(0), pl.program_id(1)
    rc = tm // n_chunks
    cur = j % 2
    prv = 1 - cur
    d = h_ref.shape[-1]

    def matmuls(zero):
        h = h_ref[...]
        for g, (w_ref, u_ref) in enumerate(zip(w_refs, u_refs)):
            if zero is None:
                u = jnp.dot(h, w_ref[...], preferred_element_type=F32)
            elif mode == "mixer":
                k0 = d - k_tail
                z = functools.reduce(lambda a, b: a + b, zero)
                u = (jnp.dot(h[:, :k0], w_ref[:k0, :], preferred_element_type=F32)
                     + jnp.dot(h[:, k0:], w_ref[k0:, :] + z, preferred_element_type=F32))
            elif g > 0:
                u = (jnp.dot(h[:, :k_tail], w_ref[:k_tail, :] + zero[-1], preferred_element_type=F32)
                     + jnp.dot(h[:, k_tail:], w_ref[k_tail:, :], preferred_element_type=F32))
            else:
                nk = d // k_tail
                per = len(zero) // nk
                u = None
                for c in range(nk):
                    z = functools.reduce(lambda a, b: a + b, zero[c * per:(c + 1) * per])
                    p = jnp.dot(h[:, c * k_tail:(c + 1) * k_tail], w_ref[c * k_tail:(c + 1) * k_tail, :] + z,
                                preferred_element_type=F32)
                    u = p if u is None else u + p
            pad = U_PAD if u_ref.shape[1] > tm else 0
            u_ref[cur, pad:pad + tm, :] = u

    def finish(lat):
        for u_ref, ue_ref in zip(conv_u, ue_refs):
            _place_edges(u_ref.at[prv], ue_ref, i, lat, rows, tm)
        zeros = []
        for c in range(n_chunks):
            lo = c * rc
            ext = pl.ds(lo, rc + 2 * U_PAD)
            if mode == "mixer":
                y = _conv_rows(conv_u[0][prv, ext, :] * conv_u[1][prv, ext, :], cw_refs[0], lat, i * tm + lo, rows)
                out = u_refs[0][prv, pl.ds(lo, rc), :] * y
            else:
                g = _conv_rows(conv_u[0][prv, ext, :], cw_refs[0], lat, i * tm + lo, rows)
                v = _conv_rows(conv_u[1][prv, ext, :], cw_refs[1], lat, i * tm + lo, rows)
                out = _silu(g) * v
            o_ref[lo:lo + rc, :] = out.astype(o_ref.dtype)
            b = pltpu.bitcast(out[0:F32_SUBLANES, :], jnp.uint32)
            zeros.append(pltpu.bitcast((b >> 16) >> 16, F32)[0:1, :].astype(BF16))
        return zeros

    def step(lat, first, last):
        zero = None if first else finish(lat)
        if not last:
            matmuls(zero)

    is_lat = i >= rows.m_ctx // tm
    pl.when(j == 0)(functools.partial(step, True, True, False))
    for lat in (False, True):
        kind = is_lat if lat else jnp.logical_not(is_lat)
        pl.when(jnp.logical_and(kind, jnp.logical_and(j > 0, j < n_tiles)))(
            functools.partial(step, lat, False, False))
        pl.when(jnp.logical_and(kind, j == n_tiles))(functools.partial(step, lat, False, True))


def _gated_conv_proj(h, w, layer, col0, conv_w, mode, rows):
    m, d = h.shape
    n_w = 3 if mode == "mixer" else 2
    n_cw = 1 if mode == "mixer" else 2
    width = conv_w.shape[-1] // n_cw
    tf = _pick(math.gcd(width, col0) if col0 else width, 256, LANE)
    tm = rows.row_tile()
    n_chunks = tm // BF16_SUBLANES if mode == "ffn" else min(16, tm // BF16_SUBLANES)
    k_tail = min(256, d // 2)
    assert d % k_tail == 0 and n_chunks % (d // k_tail) == 0
    n_tiles = width // tf
    conv_col0 = col0 + (n_w - 2) * width
    u_edge = _edge_proj(h, w, layer, conv_col0, 2 * width, tm)

    def tile(j):
        return jnp.minimum(j, n_tiles - 1)

    def done(j):
        return jnp.maximum(j - 1, 0)

    def groups(n, shape, index):
        return [pl.BlockSpec(shape, functools.partial(index, g * n_tiles)) for g in range(n)]

    kern = functools.partial(_gated_conv_kernel, mode=mode, n_tiles=n_tiles, rows=rows, tm=tm,
                             n_chunks=n_chunks, k_tail=k_tail)
    plain = pltpu.VMEM((2, tm, tf), F32)
    padded = pltpu.VMEM((2, tm + 2 * U_PAD, tf), F32)
    return pl.pallas_call(
        kern,
        grid=(m // tm, n_tiles + 1),
        in_specs=[pl.BlockSpec((tm, d), lambda i, j: (i, 0))]
        + groups(n_w, (None, d, tf), lambda off, i, j: (layer, 0, col0 // tf + off + tile(j)))
        + groups(2, (None, F32_SUBLANES, tf), lambda off, i, j: (i, 0, off + done(j)))
        + groups(n_cw, (None, 3, tf), lambda off, i, j: (layer, 0, off + done(j))),
        out_specs=pl.BlockSpec((tm, tf), lambda i, j: (i, done(j))),
        out_shape=jax.ShapeDtypeStruct((m, width), BF16),
        scratch_shapes=[plain] * (n_w - 2) + [padded, padded],
        compiler_params=_params(("arbitrary", "arbitrary")),
        name="gated_conv_" + mode,
    )(h, *([w] * n_w), u_edge, u_edge, *([conv_w] * n_cw))


def _resid_kernel(a_ref, w_ref, x_ref, gt_ref, o_ref):
    p = jnp.dot(a_ref[...], w_ref[...], preferred_element_type=F32)
    o_ref[...] = x_ref[...] + gt_ref[...] * p


def _resid_proj(a, w, layer, x, mod, gt_idx, rows, tn_pref):
    m, k = a.shape
    n = w.shape[-1]
    tm = rows.row_tile()
    tn = _pick(n, tn_pref, LANE)
    return pl.pallas_call(
        _resid_kernel,
        grid=(m // tm, n // tn),
        in_specs=[
            pl.BlockSpec((tm, k), lambda i, j: (i, 0), pipeline_mode=pl.Buffered(1)),
            pl.BlockSpec((None, k, tn), lambda i, j: (layer, 0, j)),
            pl.BlockSpec((tm, tn), lambda i, j: (i, j)),
            pl.BlockSpec((None, None, None, 1, tn),
                         lambda i, j: (layer, rows.mod_row(i, tm), gt_idx, 0, j)),
        ],
        out_specs=pl.BlockSpec((tm, tn), lambda i, j: (i, j)),
        out_shape=jax.ShapeDtypeStruct((m, n), F32),
        compiler_params=_params(("arbitrary", "arbitrary")),
        name="resid_proj",
    )(a, w, x, mod)


def _stack_heads(q_ref, kv_head, group):
    base = kv_head * group
    return jnp.concatenate(
        [q_ref[:, (base + g) * HEAD_DIM:(base + g + 1) * HEAD_DIM] for g in range(group)], axis=0)


def _sink_column(sink_ref, kv_head, group, rows_per_head):
    rg = lax.broadcasted_iota(jnp.int32, (group * rows_per_head, 1), 0) // rows_per_head
    sk = jnp.zeros((group * rows_per_head, 1), F32)
    for g in range(group):
        sk = jnp.where(rg == g, sink_ref[kv_head * group + g], sk)
    return sk


def _qk(q, k):
    return lax.dot_general(q, k, (((1,), (1,)), ((), ())), preferred_element_type=F32)


def _ctx_attn_kernel(sink_ref, q_ref, k_ref, v_ref, o_ref, *, group, n_kv, scale):
    s_len = q_ref.shape[0]
    for h in range(n_kv):
        q = _stack_heads(q_ref, h, group)
        k = k_ref[:, h * HEAD_DIM:(h + 1) * HEAD_DIM]
        v = v_ref[:, h * HEAD_DIM:(h + 1) * HEAD_DIM]
        s = _qk(q, k) * scale
        sk = _sink_column(sink_ref, h, group, s_len)
        mx = jnp.maximum(jnp.max(s, axis=-1, keepdims=True), sk)
        e = jnp.exp(s - mx)
        inv = 1.0 / (jnp.sum(e, axis=-1, keepdims=True) + jnp.exp(sk - mx))
        o = jnp.dot((e * inv).astype(BF16), v, preferred_element_type=F32)
        for g in range(group):
            c0 = (h * group + g) * HEAD_DIM
            o_ref[:, c0:c0 + HEAD_DIM] = o[g * s_len:(g + 1) * s_len].astype(o_ref.dtype)


def _ctx_attention(qkv, sink, rows, n_heads, n_kv):
    m = qkv.shape[0]
    attn_w, kv_w = n_heads * HEAD_DIM, n_kv * HEAD_DIM
    s_len = rows.ctx_len
    kern = functools.partial(_ctx_attn_kernel, group=n_heads // n_kv, n_kv=n_kv, scale=HEAD_DIM ** -0.5)
    return pl.pallas_call(
        kern,
        grid=(rows.n_ctx_seq,),
        in_specs=[
            pl.BlockSpec(memory_space=pltpu.SMEM),
            pl.BlockSpec((s_len, attn_w), lambda b: (b, 0)),
            pl.BlockSpec((s_len, kv_w), lambda b: (b, attn_w // kv_w)),
            pl.BlockSpec((s_len, kv_w), lambda b: (b, attn_w // kv_w + 1)),
        ],
        out_specs=pl.BlockSpec((s_len, attn_w), lambda b: (b, 0)),
        out_shape=jax.ShapeDtypeStruct((m, attn_w), BF16),
        compiler_params=_params(("arbitrary",)),
        name="ctx_attention",
    )(sink, qkv, qkv, qkv)


def _lat_attn_kernel(sink_ref, q_ref, k_ref, v_ref, kc_ref, vc_ref, prev_ref, o_ref, *,
                     group, n_kv, t_len, scale):
    del prev_ref
    blk = pl.program_id(1)
    band = 3 * Q_BLOCK
    start = pl.multiple_of(jnp.clip((blk - 1) * Q_BLOCK, 0, t_len - band), Q_BLOCK)
    kb = k_ref[pl.ds(start, band), :]
    vb = v_ref[pl.ds(start, band), :]
    n_rows = group * Q_BLOCK
    qpos = blk * Q_BLOCK + lax.broadcasted_iota(jnp.int32, (n_rows, band), 0) % Q_BLOCK
    kpos = start + lax.broadcasted_iota(jnp.int32, (n_rows, band), 1)
    valid = jnp.abs(kpos - qpos) <= WINDOW
    for h in range(n_kv):
        cols = slice(h * HEAD_DIM, (h + 1) * HEAD_DIM)
        q = _stack_heads(q_ref, h, group)
        s_c = _qk(q, kc_ref[:, cols]) * scale
        s_b = jnp.where(valid, _qk(q, kb[:, cols]) * scale, NEG)
        sk = _sink_column(sink_ref, h, group, Q_BLOCK)
        mx = jnp.maximum(jnp.maximum(jnp.max(s_c, axis=-1, keepdims=True),
                                     jnp.max(s_b, axis=-1, keepdims=True)), sk)
        e_c = jnp.exp(s_c - mx)
        e_b = jnp.exp(s_b - mx)
        inv = 1.0 / (jnp.sum(e_c, axis=-1, keepdims=True) + jnp.sum(e_b, axis=-1, keepdims=True)
                     + jnp.exp(sk - mx))
        o = (jnp.dot((e_c * inv).astype(BF16), vc_ref[:, cols], preferred_element_type=F32)
             + jnp.dot((e_b * inv).astype(BF16), vb[:, cols], preferred_element_type=F32))
        for g in range(group):
            c0 = (h * group + g) * HEAD_DIM
            o_ref[:, c0:c0 + HEAD_DIM] = o[g * Q_BLOCK:(g + 1) * Q_BLOCK].astype(o_ref.dtype)


def _lat_attention(qkv, cache_k, cache_v, layer, sink, attn_prev, rows, n_heads, n_kv):
    attn_w, kv_w = n_heads * HEAD_DIM, n_kv * HEAD_DIM
    t_len = rows.lat_len
    p_len = cache_k.shape[2]
    assert rows.m_ctx % t_len == 0 and t_len >= 3 * Q_BLOCK and t_len % Q_BLOCK == 0
    nblk = t_len // Q_BLOCK
    q0 = rows.m_ctx // Q_BLOCK
    s0 = rows.m_ctx // t_len
    kern = functools.partial(_lat_attn_kernel, group=n_heads // n_kv, n_kv=n_kv, t_len=t_len,
                             scale=HEAD_DIM ** -0.5)
    return pl.pallas_call(
        kern,
        grid=(rows.n_lat_seq, nblk),
        in_specs=[
            pl.BlockSpec(memory_space=pltpu.SMEM),
            pl.BlockSpec((Q_BLOCK, attn_w), lambda b, q: (q0 + b * nblk + q, 0)),
            pl.BlockSpec((t_len, kv_w), lambda b, q: (s0 + b, attn_w // kv_w)),
            pl.BlockSpec((t_len, kv_w), lambda b, q: (s0 + b, attn_w // kv_w + 1)),
            pl.BlockSpec((None, None, p_len, kv_w), lambda b, q: (b, layer, 0, 0)),
            pl.BlockSpec((None, None, p_len, kv_w), lambda b, q: (b, layer, 0, 0)),
            pl.BlockSpec(memory_space=pl.ANY),
        ],
        out_specs=pl.BlockSpec((Q_BLOCK, attn_w), lambda b, q: (q0 + b * nblk + q, 0)),
        out_shape=jax.ShapeDtypeStruct(attn_prev.shape, attn_prev.dtype),
        input_output_aliases={6: 0},
        compiler_params=_params(("arbitrary", "arbitrary")),
        name="lat_attention",
    )(sink, qkv, qkv, qkv, cache_k, cache_v, attn_prev)


def kernel(x_prompt, x_sample, cache_k, cache_v, c, c_ctx, g_mix, w_mod, b_mod, w_in, conv_w,
           g_attn_out, g_conv_out, attn_sink, w_out, g_ffn, w_up, ffn_conv_w, w_down, g_final):
    batch, seq, d = x_prompt.shape
    dec_batch, dec_seq, _ = x_sample.shape
    depth = w_in.shape[0]
    n_heads = attn_sink.shape[1]
    n_kv = cache_k.shape[3]
    attn_w, kv_w = n_heads * HEAD_DIM, n_kv * HEAD_DIM
    conv_width = d - attn_w
    d_ff = w_down.shape[1]
    qkv_w = attn_w + 2 * kv_w
    assert cache_k.shape[0] == dec_batch and cache_k.shape[4] == HEAD_DIM
    assert w_in.shape[2] == qkv_w + 3 * conv_width and w_up.shape[2] == 2 * d_ff
    assert dec_batch + 1 <= MOD_ROWS
    rows = _Rows(batch, seq, dec_batch, dec_seq)

    w_in_b, w_out_b, w_up_b, w_down_b = (w.astype(BF16) for w in (w_in, w_out, w_up, w_down))
    p_len = cache_k.shape[2]
    cache_k_b = cache_k.astype(BF16).reshape(dec_batch, depth, p_len, kv_w)
    cache_v_b = cache_v.astype(BF16).reshape(dec_batch, depth, p_len, kv_w)

    cvec = jnp.concatenate([c_ctx[None, :], c, jnp.zeros((MOD_ROWS - 1 - dec_batch, d), F32)], axis=0)
    mod = _adaln(cvec, w_mod, b_mod).reshape(depth, MOD_ROWS, 6, 1, d)
    tab = _rope_tables(dec_seq)

    x = jnp.concatenate([x_prompt.reshape(rows.m_ctx, d), x_sample.reshape(rows.m_lat, d)], axis=0)
    new_k, new_v = [], []
    for l in range(depth):
        h = _norm_mod(x, g_mix, mod, l, 1, 0, rows)
        qkv = _qkv_proj(h, w_in_b, l, qkv_w, attn_w + kv_w, tab, rows)
        conv = _gated_conv_proj(h, w_in_b, l, qkv_w, conv_w, "mixer", rows)
        attn = _ctx_attention(qkv, attn_sink[l], rows, n_heads, n_kv)
        attn = _lat_attention(qkv, cache_k_b, cache_v_b, l, attn_sink[l], attn, rows, n_heads, n_kv)
        y = _merge_norm(attn, conv, g_attn_out, g_conv_out, l)
        x = _resid_proj(y, w_out_b, l, x, mod, 2, rows, 1024)
        h = _norm_mod(x, g_ffn, mod, l, 4, 3, rows)
        act = _gated_conv_proj(h, w_up_b, l, 0, ffn_conv_w, "ffn", rows)
        x = _resid_proj(act, w_down_b, l, x, mod, 5, rows, 256)
        ctx_kv = qkv[:rows.m_ctx, attn_w:].astype(F32)
        new_k.append(ctx_kv[:, :kv_w].reshape(batch, seq, n_kv, HEAD_DIM))
        new_v.append(ctx_kv[:, kv_w:].reshape(batch, seq, n_kv, HEAD_DIM))

    y_ctx, y_lat = _final_norm(x, g_final, rows)
    y_prompt = y_ctx.reshape(batch, seq, d)
    y_sample = y_lat.reshape(dec_batch, dec_seq, d)
    return (y_prompt, y_sample, jnp.stack(new_k, axis=1), jnp.stack(new_v, axis=1))
```

```python
import functools
import math

import jax
import jax.numpy as jnp
from jax import lax
from jax.experimental import pallas as pl
from jax.experimental.pallas import tpu as pltpu

HEAD_DIM = 128
GRID_W = 64
WINDOW = 128
Q_BLOCK = 128
ROPE_THETA = 10000.0
EPS = 1e-6
NEG = -1e30

F32 = jnp.float32
BF16 = jnp.bfloat16

V7X_VMEM_BYTES = 64 * 2**20
VMEM_LIMIT_BYTES = V7X_VMEM_BYTES - 8 * 2**20
LANE = 128
F32_SUBLANES = 8
BF16_SUBLANES = 16
MOD_ROWS = F32_SUBLANES
U_PAD = F32_SUBLANES


def _pick(dim, pref, mult):
    t = (min(pref, dim) // mult) * mult
    while t > mult and dim % t:
        t -= mult
    assert t > 0 and dim % t == 0, (dim, pref, mult)
    return t


def _params(semantics):
    return pltpu.CompilerParams(dimension_semantics=semantics, vmem_limit_bytes=VMEM_LIMIT_BYTES)


def _silu(x):
    return x * (1.0 / (1.0 + jnp.exp(-x)))


class _Rows:
    def __init__(self, n_ctx_seq, ctx_len, n_lat_seq, lat_len):
        self.ctx_len, self.lat_len = ctx_len, lat_len
        self.n_ctx_seq, self.n_lat_seq = n_ctx_seq, n_lat_seq
        self.m_ctx = n_ctx_seq * ctx_len
        self.m_lat = n_lat_seq * lat_len
        self.m = self.m_ctx + self.m_lat

    def row_tile(self):
        tm = _pick(math.gcd(self.m_ctx, self.lat_len), 1024, BF16_SUBLANES)
        assert self.lat_len % tm == 0 and self.m_ctx % tm == 0
        return tm

    def mod_row(self, i, tm):
        nc = self.m_ctx // tm
        return jnp.where(i < nc, 0, 1 + (i - nc) // (self.lat_len // tm))

    def lat_block(self, i, tm):
        nc = self.m_ctx // tm
        return jnp.maximum(i - nc, 0) % (self.lat_len // tm)


def _adaln_kernel(c_ref, w_ref, b_ref, o_ref):
    s = _silu(c_ref[...]).astype(BF16)
    o_ref[...] = jnp.dot(s, w_ref[...].astype(BF16), preferred_element_type=F32) + b_ref[...]


def _adaln(cvec, w_mod, b_mod):
    depth, d, n = w_mod.shape
    tn = _pick(n, 512, LANE)
    return pl.pallas_call(
        _adaln_kernel,
        grid=(depth, n // tn),
        in_specs=[
            pl.BlockSpec((MOD_ROWS, d), lambda l, j: (0, 0)),
            pl.BlockSpec((None, d, tn), lambda l, j: (l, 0, j)),
            pl.BlockSpec((None, 1, tn), lambda l, j: (l, 0, j)),
        ],
        out_specs=pl.BlockSpec((None, MOD_ROWS, tn), lambda l, j: (l, 0, j)),
        out_shape=jax.ShapeDtypeStruct((depth, MOD_ROWS, n), F32),
        compiler_params=_params(("arbitrary", "arbitrary")),
        name="adaln",
    )(cvec, w_mod, b_mod.reshape(depth, 1, n))


def _rms(x):
    return x * lax.rsqrt(jnp.mean(x * x, axis=-1, keepdims=True) + EPS)


def _norm_mod_kernel(x_ref, g_ref, sc_ref, sh_ref, o_ref):
    y = _rms(x_ref[...]) * g_ref[...]
    o_ref[...] = (y * (1.0 + sc_ref[...]) + sh_ref[...]).astype(o_ref.dtype)


def _norm_mod(x, g, mod, layer, sc_idx, sh_idx, rows):
    m, d = x.shape
    tr = _pick(math.gcd(rows.m_ctx, rows.lat_len), 512, F32_SUBLANES)

    def mod_spec(c):
        return pl.BlockSpec((None, None, None, 1, d), lambda i: (layer, rows.mod_row(i, tr), c, 0, 0))

    return pl.pallas_call(
        _norm_mod_kernel,
        grid=(m // tr,),
        in_specs=[
            pl.BlockSpec((tr, d), lambda i: (i, 0)),
            pl.BlockSpec((None, 1, d), lambda i: (layer, 0, 0)),
            mod_spec(sc_idx),
            mod_spec(sh_idx),
        ],
        out_specs=pl.BlockSpec((tr, d), lambda i: (i, 0)),
        out_shape=jax.ShapeDtypeStruct((m, d), BF16),
        compiler_params=_params(("arbitrary",)),
        name="norm_mod",
    )(x, g.reshape(g.shape[0], 1, d), mod, mod)


def _final_norm_kernel(x_ref, g_ref, oc_ref, ol_ref, *, n_ctx_blocks):
    y = _rms(x_ref[...]) * g_ref[...]
    i = pl.program_id(0)

    @pl.when(i < n_ctx_blocks)
    def _():
        oc_ref[...] = y

    @pl.when(i >= n_ctx_blocks)
    def _():
        ol_ref[...] = y


def _final_norm(x, g, rows):
    m, d = x.shape
    tr = _pick(math.gcd(rows.m_ctx, rows.m_lat), 256, F32_SUBLANES)
    nc = rows.m_ctx // tr
    return pl.pallas_call(
        functools.partial(_final_norm_kernel, n_ctx_blocks=nc),
        grid=(m // tr,),
        in_specs=[pl.BlockSpec((tr, d), lambda i: (i, 0)), pl.BlockSpec((1, d), lambda i: (0, 0))],
        out_specs=[pl.BlockSpec((tr, d), lambda i: (jnp.minimum(i, nc - 1), 0)),
                   pl.BlockSpec((tr, d), lambda i: (jnp.maximum(i - nc, 0), 0))],
        out_shape=[jax.ShapeDtypeStruct((rows.m_ctx, d), F32), jax.ShapeDtypeStruct((rows.m_lat, d), F32)],
        compiler_params=_params(("arbitrary",)),
        name="final_norm",
    )(x, g.reshape(1, d))


def _merge_norm_kernel(a_ref, c_ref, ga_ref, gc_ref, o_ref):
    wa = a_ref.shape[-1]
    o_ref[:, :wa] = (_rms(a_ref[...].astype(F32)) * ga_ref[...]).astype(o_ref.dtype)
    o_ref[:, wa:] = (_rms(c_ref[...].astype(F32)) * gc_ref[...]).astype(o_ref.dtype)


def _merge_norm(attn, conv, g_attn, g_conv, layer):
    m, wa = attn.shape
    wc = conv.shape[1]
    tr = _pick(m, 512, BF16_SUBLANES)
    return pl.pallas_call(
        _merge_norm_kernel,
        grid=(m // tr,),
        in_specs=[
            pl.BlockSpec((tr, wa), lambda i: (i, 0)),
            pl.BlockSpec((tr, wc), lambda i: (i, 0)),
            pl.BlockSpec((None, 1, wa), lambda i: (layer, 0, 0)),
            pl.BlockSpec((None, 1, wc), lambda i: (layer, 0, 0)),
        ],
        out_specs=pl.BlockSpec((tr, wa + wc), lambda i: (i, 0)),
        out_shape=jax.ShapeDtypeStruct((m, wa + wc), BF16),
        compiler_params=_params(("arbitrary",)),
        name="merge_norm",
    )(attn, conv, g_attn.reshape(-1, 1, wa), g_conv.reshape(-1, 1, wc))


def _rope_tables(t_len):
    pos = jnp.arange(t_len)
    row = (pos // GRID_W).astype(F32)
    col = (pos % GRID_W).astype(F32)
    n_freq = HEAD_DIM // 4
    inv = ROPE_THETA ** (-jnp.arange(n_freq, dtype=F32) / n_freq)
    ar = row[:, None] * inv
    ac = col[:, None] * inv
    ang = jnp.concatenate([ar, ar, ac, ac], axis=-1)
    cos, sin = jnp.cos(ang), jnp.sin(ang)
    lower = (jnp.arange(HEAD_DIM) % (2 * n_freq)) < n_freq
    return jnp.concatenate([cos, jnp.where(lower, -sin, 0.0), jnp.where(lower, 0.0, sin)], axis=-1)


def _qkv_kernel(h_ref, w_ref, tab_ref, o_ref, *, n_ctx_blocks, rope_heads):
    i, j = pl.program_id(0), pl.program_id(1)
    acc = jnp.dot(h_ref[...], w_ref[...], preferred_element_type=F32)
    heads = o_ref.shape[-1] // HEAD_DIM

    def emit(n_rope):
        for hc in range(heads):
            cols = slice(hc * HEAD_DIM, (hc + 1) * HEAD_DIM)
            a = acc[:, cols]
            if hc < n_rope:
                cos = tab_ref[:, :HEAD_DIM]
                sin_up = tab_ref[:, HEAD_DIM:2 * HEAD_DIM]
                sin_dn = tab_ref[:, 2 * HEAD_DIM:]
                a = (a * cos + pltpu.roll(a, HEAD_DIM - HEAD_DIM // 4, 1) * sin_up
                     + pltpu.roll(a, HEAD_DIM // 4, 1) * sin_dn)
            o_ref[:, cols] = a.astype(o_ref.dtype)

    is_lat = i >= n_ctx_blocks
    for n_rope in sorted(set(rope_heads) - {0}):
        tiles = [jj for jj, n in enumerate(rope_heads) if n == n_rope]
        hit = functools.reduce(jnp.logical_or, [j == jj for jj in tiles])
        pl.when(jnp.logical_and(is_lat, hit))(functools.partial(emit, n_rope))
    plain = [jj for jj, n in enumerate(rope_heads) if n == 0]
    no_rope = jnp.logical_not(is_lat)
    if plain:
        no_rope = jnp.logical_or(no_rope, functools.reduce(jnp.logical_or, [j == jj for jj in plain]))
    pl.when(no_rope)(functools.partial(emit, 0))


def _qkv_proj(h, w_qkv, layer, n_cols, n_rope_cols, tab, rows):
    m, d = h.shape
    tm = rows.row_tile()
    tn = _pick(n_cols, 1024, HEAD_DIM)
    n_tiles = n_cols // tn
    rope_heads = tuple(min(max(n_rope_cols - jj * tn, 0), tn) // HEAD_DIM for jj in range(n_tiles))
    kern = functools.partial(_qkv_kernel, n_ctx_blocks=rows.m_ctx // tm, rope_heads=rope_heads)
    return pl.pallas_call(
        kern,
        grid=(m // tm, n_tiles),
        in_specs=[
            pl.BlockSpec((tm, d), lambda i, j: (i, 0)),
            pl.BlockSpec((None, d, tn), lambda i, j: (layer, 0, j)),
            pl.BlockSpec((tm, 3 * HEAD_DIM), lambda i, j: (rows.lat_block(i, tm), 0)),
        ],
        out_specs=pl.BlockSpec((tm, tn), lambda i, j: (i, j)),
        out_shape=jax.ShapeDtypeStruct((m, n_cols), BF16),
        compiler_params=_params(("arbitrary", "arbitrary")),
        name="qkv_proj",
    )(h, w_qkv, tab)


def _edge_kernel(h_ref, w_ref, o_ref):
    o_ref[...] = jnp.dot(h_ref[...], w_ref[...], preferred_element_type=F32)


def _edge_proj(h, w, layer, col0, n_cols, tm):
    m, d = h.shape
    nblk = m // tm
    idx = []
    for i in range(nblk):
        idx += [max(i * tm - 1, 0), min((i + 1) * tm, m - 1)]
    n_rows = -(-len(idx) // BF16_SUBLANES) * BF16_SUBLANES
    idx += [0] * (n_rows - len(idx))
    h_edge = jnp.take(h, jnp.asarray(idx, jnp.int32), axis=0)
    tn = _pick(math.gcd(n_cols, col0) if col0 else n_cols, 512, LANE)
    u = pl.pallas_call(
        _edge_kernel,
        grid=(n_cols // tn,),
        in_specs=[
            pl.BlockSpec((n_rows, d), lambda j: (0, 0)),
            pl.BlockSpec((None, d, tn), lambda j: (layer, 0, col0 // tn + j)),
        ],
        out_specs=pl.BlockSpec((n_rows, tn), lambda j: (0, j)),
        out_shape=jax.ShapeDtypeStruct((n_rows, n_cols), F32),
        compiler_params=_params(("arbitrary",)),
        name="edge_proj",
    )(h_edge, w)
    return jnp.pad(u[:2 * nblk].reshape(nblk, 2, n_cols), ((0, 0), (0, F32_SUBLANES - 2), (0, 0)))


def _place_edges(u_ref, ue_ref, i, lat, rows, tm):
    before, after = ue_ref[0:1, :], ue_ref[1:2, :]
    if lat:
        seq_row = ((i - rows.m_ctx // tm) * tm) % rows.lat_len
        before = jnp.where(seq_row != 0, before, 0.0)
        after = jnp.where(seq_row + tm != rows.lat_len, after, 0.0)
    u_ref[U_PAD - 1:U_PAD, :] = before
    u_ref[U_PAD + tm:U_PAD + tm + 1, :] = after


def _conv_rows(x, cw_ref, lat, row0, rows):
    n_ext = x.shape[0]
    rc = n_ext - 2 * U_PAD
    prev = pltpu.roll(x, 1, 0)[U_PAD:U_PAD + rc]
    nxt = pltpu.roll(x, n_ext - 1, 0)[U_PAD:U_PAD + rc]
    if not lat:
        pos = (row0 + lax.broadcasted_iota(jnp.int32, (rc, 1), 0)) % rows.ctx_len
        prev = jnp.where(pos != 0, prev, 0.0)
        nxt = jnp.where(pos != rows.ctx_len - 1, nxt, 0.0)
    return cw_ref[0:1, :] * prev + cw_ref[1:2, :] * x[U_PAD:U_PAD + rc] + cw_ref[2:3, :] * nxt


def _gated_conv_kernel(h_ref, *refs, mode, n_tiles, rows, tm, n_chunks, k_tail):
    n_w = 3 if mode == "mixer" else 2
    w_refs = refs[:n_w]
    ue_refs = refs[n_w:n_w + 2]
    cw_refs = refs[n_w + 2:-1 - n_w]
    o_ref = refs[-1 - n_w]
    u_refs = refs[-n_w:]
    conv_u = u_refs[-2:]
    i, j = pl.program_id(0), pl.program_id(1)
    rc = tm // n_chunks
    cur = j % 2
    prv = 1 - cur
    d = h_ref.shape[-1]

    def matmuls(zero):
        h = h_ref[...]
        for g, (w_ref, u_ref) in enumerate(zip(w_refs, u_refs)):
            if zero is None:
                u = jnp.dot(h, w_ref[...], preferred_element_type=F32)
            elif mode == "mixer":
                k0 = d - k_tail
                z = functools.reduce(lambda a, b: a + b, zero)
                u = (jnp.dot(h[:, :k0], w_ref[:k0, :], preferred_element_type=F32)
                     + jnp.dot(h[:, k0:], w_ref[k0:, :] + z, preferred_element_type=F32))
            elif g > 0:
                u = (jnp.dot(h[:, :k_tail], w_ref[:k_tail, :] + zero[-1], preferred_element_type=F32)
                     + jnp.dot(h[:, k_tail:], w_ref[k_tail:, :], preferred_element_type=F32))
            else:
                nk = d // k_tail
                per = len(zero) // nk
                u = None
                for c in range(nk):
                    z = functools.reduce(lambda a, b: a + b, zero[c * per:(c + 1) * per])
                    p = jnp.dot(h[:, c * k_tail:(c + 1) * k_tail], w_ref[c * k_tail:(c + 1) * k_tail, :] + z,
                                preferred_element_type=F32)
                    u = p if u is None else u + p
            pad = U_PAD if u_ref.shape[1] > tm else 0
            u_ref[cur, pad:pad + tm, :] = u

    def finish(lat):
        for u_ref, ue_ref in zip(conv_u, ue_refs):
            _place_edges(u_ref.at[prv], ue_ref, i, lat, rows, tm)
        zeros = []
        for c in range(n_chunks):
            lo = c * rc
            ext = pl.ds(lo, rc + 2 * U_PAD)
            if mode == "mixer":
                y = _conv_rows(conv_u[0][prv, ext, :] * conv_u[1][prv, ext, :], cw_refs[0], lat, i * tm + lo, rows)
                out = u_refs[0][prv, pl.ds(lo, rc), :] * y
            else:
                g = _conv_rows(conv_u[0][prv, ext, :], cw_refs[0], lat, i * tm + lo, rows)
                v = _conv_rows(conv_u[1][prv, ext, :], cw_refs[1], lat, i * tm + lo, rows)
                out = _silu(g) * v
            o_ref[lo:lo + rc, :] = out.astype(o_ref.dtype)
            b = pltpu.bitcast(out[0:F32_SUBLANES, :], jnp.uint32)
            zeros.append(pltpu.bitcast((b >> 16) >> 16, F32)[0:1, :].astype(BF16))
        return zeros

    def step(lat, first, last):
        zero = None if first else finish(lat)
        if not last:
            matmuls(zero)

    is_lat = i >= rows.m_ctx // tm
    pl.when(j == 0)(functools.partial(step, True, True, False))
    for lat in (False, True):
        kind = is_lat if lat else jnp.logical_not(is_lat)
        pl.when(jnp.logical_and(kind, jnp.logical_and(j > 0, j < n_tiles)))(
            functools.partial(step, lat, False, False))
        pl.when(jnp.logical_and(kind, j == n_tiles))(functools.partial(step, lat, False, True))


def _gated_conv_proj(h, w, layer, col0, conv_w, mode, rows):
    m, d = h.shape
    n_w = 3 if mode == "mixer" else 2
    n_cw = 1 if mode == "mixer" else 2
    width = conv_w.shape[-1] // n_cw
    tf = _pick(math.gcd(width, col0) if col0 else width, 256, LANE)
    tm = rows.row_tile()
    n_chunks = tm // BF16_SUBLANES if mode == "ffn" else min(16, tm // BF16_SUBLANES)
    k_tail = min(256, d // 2)
    assert d % k_tail == 0 and n_chunks % (d // k_tail) == 0
    n_tiles = width // tf
    conv_col0 = col0 + (n_w - 2) * width
    u_edge = _edge_proj(h, w, layer, conv_col0, 2 * width, tm)

    def tile(j):
        return jnp.minimum(j, n_tiles - 1)

    def done(j):
        return jnp.maximum(j - 1, 0)

    def groups(n, shape, index):
        return [pl.BlockSpec(shape, functools.partial(index, g * n_tiles)) for g in range(n)]

    kern = functools.partial(_gated_conv_kernel, mode=mode, n_tiles=n_tiles, rows=rows, tm=tm,
                             n_chunks=n_chunks, k_tail=k_tail)
    plain = pltpu.VMEM((2, tm, tf), F32)
    padded = pltpu.VMEM((2, tm + 2 * U_PAD, tf), F32)
    return pl.pallas_call(
        kern,
        grid=(m // tm, n_tiles + 1),
        in_specs=[pl.BlockSpec((tm, d), lambda i, j: (i, 0))]
        + groups(n_w, (None, d, tf), lambda off, i, j: (layer, 0, col0 // tf + off + tile(j)))
        + groups(2, (None, F32_SUBLANES, tf), lambda off, i, j: (i, 0, off + done(j)))
        + groups(n_cw, (None, 3, tf), lambda off, i, j: (layer, 0, off + done(j))),
        out_specs=pl.BlockSpec((tm, tf), lambda i, j: (i, done(j))),
        out_shape=jax.ShapeDtypeStruct((m, width), BF16),
        scratch_shapes=[plain] * (n_w - 2) + [padded, padded],
        compiler_params=_params(("arbitrary", "arbitrary")),
        name="gated_conv_" + mode,
    )(h, *([w] * n_w), u_edge, u_edge, *([conv_w] * n_cw))


def _resid_kernel(a_ref, w_ref, x_ref, gt_ref, o_ref):
    p = jnp.dot(a_ref[...], w_ref[...], preferred_element_type=F32)
    o_ref[...] = x_ref[...] + gt_ref[...] * p


def _resid_proj(a, w, layer, x, mod, gt_idx, rows, tn_pref):
    m, k = a.shape
    n = w.shape[-1]
    tm = rows.row_tile()
    tn = _pick(n, tn_pref, LANE)
    a_mode = {} if 2 * tm * k * a.dtype.itemsize <= VMEM_LIMIT_BYTES // 3 else {"pipeline_mode": pl.Buffered(1)}
    return pl.pallas_call(
        _resid_kernel,
        grid=(m // tm, n // tn),
        in_specs=[
            pl.BlockSpec((tm, k), lambda i, j: (i, 0), **a_mode),
            pl.BlockSpec((None, k, tn), lambda i, j: (layer, 0, j)),
            pl.BlockSpec((tm, tn), lambda i, j: (i, j)),
            pl.BlockSpec((None, None, None, 1, tn),
                         lambda i, j: (layer, rows.mod_row(i, tm), gt_idx, 0, j)),
        ],
        out_specs=pl.BlockSpec((tm, tn), lambda i, j: (i, j)),
        out_shape=jax.ShapeDtypeStruct((m, n), F32),
        compiler_params=_params(("arbitrary", "arbitrary")),
        name="resid_proj",
    )(a, w, x, mod)


def _stack_heads(q_ref, kv_head, group):
    base = kv_head * group
    return jnp.concatenate(
        [q_ref[:, (base + g) * HEAD_DIM:(base + g + 1) * HEAD_DIM] for g in range(group)], axis=0)


def _sink_column(sink_ref, kv_head, group, rows_per_head):
    rg = lax.broadcasted_iota(jnp.int32, (group * rows_per_head, 1), 0) // rows_per_head
    sk = jnp.zeros((group * rows_per_head, 1), F32)
    for g in range(group):
        sk = jnp.where(rg == g, sink_ref[kv_head * group + g], sk)
    return sk


def _qk(q, k):
    return lax.dot_general(q, k, (((1,), (1,)), ((), ())), preferred_element_type=F32)


def _ctx_attn_kernel(sink_ref, q_ref, k_ref, v_ref, o_ref, *, group, n_kv, scale):
    s_len = q_ref.shape[0]
    for h in range(n_kv):
        q = _stack_heads(q_ref, h, group)
        k = k_ref[:, h * HEAD_DIM:(h + 1) * HEAD_DIM]
        v = v_ref[:, h * HEAD_DIM:(h + 1) * HEAD_DIM]
        s = _qk(q, k) * scale
        sk = _sink_column(sink_ref, h, group, s_len)
        mx = jnp.maximum(jnp.max(s, axis=-1, keepdims=True), sk)
        e = jnp.exp(s - mx)
        inv = 1.0 / (jnp.sum(e, axis=-1, keepdims=True) + jnp.exp(sk - mx))
        o = jnp.dot((e * inv).astype(BF16), v, preferred_element_type=F32)
        for g in range(group):
            c0 = (h * group + g) * HEAD_DIM
            o_ref[:, c0:c0 + HEAD_DIM] = o[g * s_len:(g + 1) * s_len].astype(o_ref.dtype)


def _ctx_attention(qkv, sink, rows, n_heads, n_kv):
    m = qkv.shape[0]
    attn_w, kv_w = n_heads * HEAD_DIM, n_kv * HEAD_DIM
    s_len = rows.ctx_len
    kern = functools.partial(_ctx_attn_kernel, group=n_heads // n_kv, n_kv=n_kv, scale=HEAD_DIM ** -0.5)
    return pl.pallas_call(
        kern,
        grid=(rows.n_ctx_seq,),
        in_specs=[
            pl.BlockSpec(memory_space=pltpu.SMEM),
            pl.BlockSpec((s_len, attn_w), lambda b: (b, 0)),
            pl.BlockSpec((s_len, kv_w), lambda b: (b, attn_w // kv_w)),
            pl.BlockSpec((s_len, kv_w), lambda b: (b, attn_w // kv_w + 1)),
        ],
        out_specs=pl.BlockSpec((s_len, attn_w), lambda b: (b, 0)),
        out_shape=jax.ShapeDtypeStruct((m, attn_w), BF16),
        compiler_params=_params(("arbitrary",)),
        name="ctx_attention",
    )(sink, qkv, qkv, qkv)


def _lat_attn_kernel(sink_ref, q_ref, k_ref, v_ref, kc_ref, vc_ref, prev_ref, o_ref, *,
                     group, n_kv, t_len, scale):
    del prev_ref
    blk = pl.program_id(1)
    band = 3 * Q_BLOCK
    start = pl.multiple_of(jnp.clip((blk - 1) * Q_BLOCK, 0, t_len - band), Q_BLOCK)
    kb = k_ref[pl.ds(start, band), :]
    vb = v_ref[pl.ds(start, band), :]
    n_rows = group * Q_BLOCK
    qpos = blk * Q_BLOCK + lax.broadcasted_iota(jnp.int32, (n_rows, band), 0) % Q_BLOCK
    kpos = start + lax.broadcasted_iota(jnp.int32, (n_rows, band), 1)
    valid = jnp.abs(kpos - qpos) <= WINDOW
    for h in range(n_kv):
        cols = slice(h * HEAD_DIM, (h + 1) * HEAD_DIM)
        q = _stack_heads(q_ref, h, group)
        s_c = _qk(q, kc_ref[:, cols]) * scale
        s_b = jnp.where(valid, _qk(q, kb[:, cols]) * scale, NEG)
        sk = _sink_column(sink_ref, h, group, Q_BLOCK)
        mx = jnp.maximum(jnp.maximum(jnp.max(s_c, axis=-1, keepdims=True),
                                     jnp.max(s_b, axis=-1, keepdims=True)), sk)
        e_c = jnp.exp(s_c - mx)
        e_b = jnp.exp(s_b - mx)
        inv = 1.0 / (jnp.sum(e_c, axis=-1, keepdims=True) + jnp.sum(e_b, axis=-1, keepdims=True)
                     + jnp.exp(sk - mx))
        o = (jnp.dot((e_c * inv).astype(BF16), vc_ref[:, cols], preferred_element_type=F32)
             + jnp.dot((e_b * inv).astype(BF16), vb[:, cols], preferred_element_type=F32))
        for g in range(group):
            c0 = (h * group + g) * HEAD_DIM
            o_ref[:, c0:c0 + HEAD_DIM] = o[g * Q_BLOCK:(g + 1) * Q_BLOCK].astype(o_ref.dtype)


def _lat_attention(qkv, cache_k, cache_v, layer, sink, attn_prev, rows, n_heads, n_kv):
    attn_w, kv_w = n_heads * HEAD_DIM, n_kv * HEAD_DIM
    t_len = rows.lat_len
    p_len = cache_k.shape[2]
    assert rows.m_ctx % t_len == 0 and t_len >= 3 * Q_BLOCK and t_len % Q_BLOCK == 0
    nblk = t_len // Q_BLOCK
    q0 = rows.m_ctx // Q_BLOCK
    s0 = rows.m_ctx // t_len
    kern = functools.partial(_lat_attn_kernel, group=n_heads // n_kv, n_kv=n_kv, t_len=t_len,
                             scale=HEAD_DIM ** -0.5)
    return pl.pallas_call(
        kern,
        grid=(rows.n_lat_seq, nblk),
        in_specs=[
            pl.BlockSpec(memory_space=pltpu.SMEM),
            pl.BlockSpec((Q_BLOCK, attn_w), lambda b, q: (q0 + b * nblk + q, 0)),
            pl.BlockSpec((t_len, kv_w), lambda b, q: (s0 + b, attn_w // kv_w)),
            pl.BlockSpec((t_len, kv_w), lambda b, q: (s0 + b, attn_w // kv_w + 1)),
            pl.BlockSpec((None, None, p_len, kv_w), lambda b, q: (b, layer, 0, 0)),
            pl.BlockSpec((None, None, p_len, kv_w), lambda b, q: (b, layer, 0, 0)),
            pl.BlockSpec(memory_space=pl.ANY),
        ],
        out_specs=pl.BlockSpec((Q_BLOCK, attn_w), lambda b, q: (q0 + b * nblk + q, 0)),
        out_shape=jax.ShapeDtypeStruct(attn_prev.shape, attn_prev.dtype),
        input_output_aliases={6: 0},
        compiler_params=_params(("arbitrary", "arbitrary")),
        name="lat_attention",
    )(sink, qkv, qkv, qkv, cache_k, cache_v, attn_prev)


def kernel(x_prompt, x_sample, cache_k, cache_v, c, c_ctx, g_mix, w_mod, b_mod, w_in, conv_w,
           g_attn_out, g_conv_out, attn_sink, w_out, g_ffn, w_up, ffn_conv_w, w_down, g_final):
    batch, seq, d = x_prompt.shape
    dec_batch, dec_seq, _ = x_sample.shape
    depth = w_in.shape[0]
    n_heads = attn_sink.shape[1]
    n_kv = cache_k.shape[3]
    attn_w, kv_w = n_heads * HEAD_DIM, n_kv * HEAD_DIM
    conv_width = d - attn_w
    d_ff = w_down.shape[1]
    qkv_w = attn_w + 2 * kv_w
    assert cache_k.shape[0] == dec_batch and cache_k.shape[4] == HEAD_DIM
    assert w_in.shape[2] == qkv_w + 3 * conv_width and w_up.shape[2] == 2 * d_ff
    assert dec_batch + 1 <= MOD_ROWS
    rows = _Rows(batch, seq, dec_batch, dec_seq)

    w_in_b, w_out_b, w_up_b, w_down_b = (w.astype(BF16) for w in (w_in, w_out, w_up, w_down))
    p_len = cache_k.shape[2]
    cache_k_b = cache_k.astype(BF16).reshape(dec_batch, depth, p_len, kv_w)
    cache_v_b = cache_v.astype(BF16).reshape(dec_batch, depth, p_len, kv_w)

    cvec = jnp.concatenate([c_ctx[None, :], c, jnp.zeros((MOD_ROWS - 1 - dec_batch, d), F32)], axis=0)
    mod = _adaln(cvec, w_mod, b_mod).reshape(depth, MOD_ROWS, 6, 1, d)
    tab = _rope_tables(dec_seq)

    x = jnp.concatenate([x_prompt.reshape(rows.m_ctx, d), x_sample.reshape(rows.m_lat, d)], axis=0)
    new_k, new_v = [], []
    for l in range(depth):
        h = _norm_mod(x, g_mix, mod, l, 1, 0, rows)
        qkv = _qkv_proj(h, w_in_b, l, qkv_w, attn_w + kv_w, tab, rows)
        conv = _gated_conv_proj(h, w_in_b, l, qkv_w, conv_w, "mixer", rows)
        attn = _ctx_attention(qkv, attn_sink[l], rows, n_heads, n_kv)
        attn = _lat_attention(qkv, cache_k_b, cache_v_b, l, attn_sink[l], attn, rows, n_heads, n_kv)
        y = _merge_norm(attn, conv, g_attn_out, g_conv_out, l)
        x = _resid_proj(y, w_out_b, l, x, mod, 2, rows, 1024)
        h = _norm_mod(x, g_ffn, mod, l, 4, 3, rows)
        act = _gated_conv_proj(h, w_up_b, l, 0, ffn_conv_w, "ffn", rows)
        x = _resid_proj(act, w_down_b, l, x, mod, 5, rows, 256)
        ctx_kv = qkv[:rows.m_ctx, attn_w:].astype(F32)
        new_k.append(ctx_kv[:, :kv_w].reshape(batch, seq, n_kv, HEAD_DIM))
        new_v.append(ctx_kv[:, kv_w:].reshape(batch, seq, n_kv, HEAD_DIM))

    y_ctx, y_lat = _final_norm(x, g_final, rows)
    y_prompt = y_ctx.reshape(batch, seq, d)
    y_sample = y_lat.reshape(dec_batch, dec_seq, d)
    return (y_prompt, y_sample, jnp.stack(new_k, axis=1), jnp.stack(new_v, axis=1))
```
